```python
import math
import jax, jax.numpy as jnp
from jax import lax
import numpy as np

D_MODEL = 4096
BATCH = 8
SEQ = 2048
DEPTH = 4

HEAD_DIM = 128
MIX_WIDTH = D_MODEL // 2
MLSTM_HEADS = MIX_WIDTH // (4 * HEAD_DIM)
NSA_HEADS = MIX_WIDTH // (2 * HEAD_DIM)
RET_HEADS = MIX_WIDTH // (4 * HEAD_DIM)
NSA_KV_GROUPS = 2
NSA_GROUP_SIZE = NSA_HEADS // NSA_KV_GROUPS
MLSTM_WIDTH = MLSTM_HEADS * HEAD_DIM
NSA_WIDTH = NSA_HEADS * HEAD_DIM
RET_WIDTH = RET_HEADS * HEAD_DIM
NSA_KV_WIDTH = NSA_KV_GROUPS * HEAD_DIM
IN_COLS = 4 * MLSTM_WIDTH + 2 * MLSTM_HEADS + NSA_WIDTH + 6 * NSA_KV_WIDTH + 3 * NSA_HEADS + 4 * RET_WIDTH

MLSTM_CHUNK = 64
CONV_WIDTH = 4
RET_CHUNK = 128
CMP_BLOCK = 32
CMP_STRIDE = 16
SEL_BLOCK = 64
SEL_TOPK = 8
SEL_Q_BLOCK = 64
WINDOW = 512
WIN_Q_BLOCK = 128
REL_BUCKETS = 32
REL_MAX_DIST = 128
D_FF = 4 * D_MODEL
GATE_RANK = D_MODEL // 4
N_BRANCH = 3
EPS = 1e-6
NEG_INF = -1e30

kernel_name = "hybrid_mlstm_nsa_retention_block"


def rms_norm(x, gain):
    x32 = x.astype(jnp.float32)
    y = x32 * lax.rsqrt(jnp.mean(jnp.square(x32), axis=-1, keepdims=True) + EPS)
    return (y * gain).astype(x.dtype)


def head_norm(h, gain):
    h = h.astype(jnp.float32)
    mu = jnp.mean(h, axis=-1, keepdims=True)
    var = jnp.mean(jnp.square(h - mu), axis=-1, keepdims=True)
    y = (h - mu) * lax.rsqrt(var + EPS)
    return y.reshape(h.shape[0], h.shape[1], -1) * gain


def causal_depthwise_conv(x, w):
    k_width, c = w.shape
    return lax.conv_general_dilated(x, w[:, None, :].astype(x.dtype), window_strides=(1,),
                                    padding=[(k_width - 1, 0)], dimension_numbers=("NWC", "WIO", "NWC"),
                                    feature_group_count=c)


def rope(x, pos):
    half = x.shape[-1] // 2
    inv_freq = 1.0 / (10000.0 ** jnp.linspace(0.0, 1.0, half))
    ang = pos.astype(jnp.float32)[:, None] * inv_freq[None, :]
    cos, sin = jnp.cos(ang), jnp.sin(ang)
    x1, x2 = x[..., :half], x[..., half:]
    return jnp.concatenate([x1 * cos - x2 * sin, x1 * sin + x2 * cos], axis=-1)


def rel_bucket(dist):
    n = jnp.maximum(dist, 0)
    exact = REL_BUCKETS // 2
    log_ratio = jnp.log(jnp.maximum(n, 1).astype(jnp.float32) / exact) / math.log(REL_MAX_DIST / exact)
    large = jnp.minimum(exact + (log_ratio * (REL_BUCKETS - exact)).astype(jnp.int32), REL_BUCKETS - 1)
    return jnp.where(n < exact, n, large)


def split_in_proj(z):
    sizes = ([MLSTM_WIDTH] * 4 + [MLSTM_HEADS] * 2 + [NSA_WIDTH] + [NSA_KV_WIDTH] * 6
             + [3 * NSA_HEADS] + [RET_WIDTH] * 4)
    offsets = [int(o) for o in np.cumsum(sizes)[:-1]]
    return jnp.split(z, offsets, axis=-1)


def mlstm_chunkwise(q, k, v, i_pre, f_pre):
    b, h, s, dh = q.shape
    l = MLSTM_CHUNK
    n = s // l
    q = q.reshape(b, h, n, l, dh)
    k = k.reshape(b, h, n, l, dh) * (dh ** -0.5)
    v = v.reshape(b, h, n, l, dh)
    i_pre = i_pre.reshape(b, h, n, l)
    a = jnp.cumsum(jax.nn.log_sigmoid(f_pre).reshape(b, h, n, l), axis=-1)
    g = a[..., -1]
    w = g[..., None] - a + i_pre
    m_loc = jnp.max(w, axis=-1)
    e = jnp.exp(w - m_loc[..., None])
    c_chunk = jnp.einsum("bhnl,bhnlv,bhnlk->nbhvk", e, v, k)
    n_chunk = jnp.einsum("bhnl,bhnlk->nbhk", e, k)

    def step(carry, inp):
        c, nv, m = carry
        c_k, n_k, m_k, g_k = inp
        m_new = jnp.maximum(g_k + m, m_k)
        a_old = jnp.exp(g_k + m - m_new)
        a_new = jnp.exp(m_k - m_new)
        c_new = a_old[..., None, None] * c + a_new[..., None, None] * c_k
        n_new = a_old[..., None] * nv + a_new[..., None] * n_k
        return (c_new, n_new, m_new), (c, nv, m)

    init = (jnp.zeros((b, h, dh, dh), jnp.float32), jnp.zeros((b, h, dh), jnp.float32),
            jnp.zeros((b, h), jnp.float32))
    xs = (c_chunk, n_chunk, m_loc.transpose(2, 0, 1), g.transpose(2, 0, 1))
    _, (c_prev, n_prev, m_prev) = lax.scan(step, init, xs)
    c_prev = c_prev.transpose(1, 2, 0, 3, 4)
    n_prev = n_prev.transpose(1, 2, 0, 3)
    m_prev = m_prev.transpose(1, 2, 0)

    causal = jnp.tril(jnp.ones((l, l), dtype=bool))
    log_d = jnp.where(causal, a[..., :, None] - a[..., None, :] + i_pre[..., None, :], -jnp.inf)
    log_inter = a + m_prev[..., None]
    m_row = jnp.maximum(log_inter, jnp.max(log_d, axis=-1))
    inter = jnp.exp(log_inter - m_row)
    sc = jnp.einsum("bhnik,bhnjk->bhnij", q, k) * jnp.exp(log_d - m_row[..., None])
    num = inter[..., None] * jnp.einsum("bhnik,bhnvk->bhniv", q, c_prev) + jnp.einsum("bhnij,bhnjv->bhniv", sc, v)
    den = inter * jnp.einsum("bhnik,bhnk->bhni", q, n_prev) + jnp.sum(sc, axis=-1)
    out = num / jnp.maximum(jnp.abs(den), jnp.exp(-m_row))[..., None]
    return out.reshape(b, h, s, dh)


def retention_chunkwise(q, k, v):
    b, h, s, dh = q.shape
    l = RET_CHUNK
    n = s // l
    log_gamma = jnp.log1p(-jnp.exp2(-5.0 - jnp.arange(h, dtype=jnp.float32)))
    idx = jnp.arange(l, dtype=jnp.float32)
    rel = idx[:, None] - idx[None, :]
    decay = jnp.where(rel >= 0, jnp.exp(log_gamma[:, None, None] * jnp.maximum(rel, 0.0)), 0.0)
    xi = jnp.exp(log_gamma[:, None] * (idx + 1.0))
    zeta = jnp.exp(log_gamma[:, None] * (l - 1.0 - idx))
    chunk_decay = jnp.exp(log_gamma * l)
    q = q.reshape(b, h, n, l, dh)
    k = k.reshape(b, h, n, l, dh) * (dh ** -0.5)
    v = v.reshape(b, h, n, l, dh)
    kv_chunk = jnp.einsum("hl,bhnlk,bhnlv->nbhkv", zeta, k, v)

    def step(r, kv_n):
        return chunk_decay[:, None, None] * r + kv_n, r

    _, r_prev = lax.scan(step, jnp.zeros((b, h, dh, dh), jnp.float32), kv_chunk)
    r_prev = r_prev.transpose(1, 2, 0, 3, 4)
    inner = jnp.einsum("bhnik,bhnjk->bhnij", q, k) * decay[:, None]
    out = (jnp.einsum("bhnij,bhnjv->bhniv", inner, v)
           + xi[:, None, :, None] * jnp.einsum("bhnik,bhnkv->bhniv", q, r_prev))
    return out.reshape(b, h, s, dh)


def nsa_attention(q, k_cmp_raw, v_cmp_raw, k_slc, v_slc, k_win, v_win, gate_pre, rel_bias,
                  cmp_pos_k, cmp_w1_k, cmp_w2_k, cmp_pos_v, cmp_w1_v, cmp_w2_v):
    b, s, _ = q.shape
    g_n, r_n, dh = NSA_KV_GROUPS, NSA_GROUP_SIZE, HEAD_DIM
    f32 = jnp.float32
    q = q.astype(f32).reshape(b, s, g_n, r_n, dh).transpose(0, 2, 3, 1, 4) * (dh ** -0.5)

    def kv(t):
        return t.astype(f32).reshape(b, s, g_n, dh).transpose(0, 2, 1, 3)

    tbl = rel_bias.astype(f32).T.reshape(g_n, r_n, REL_BUCKETS)
    t_pos = jnp.arange(s)

    n_cmp = (s - CMP_BLOCK) // CMP_STRIDE + 1
    starts = jnp.arange(n_cmp) * CMP_STRIDE
    win_idx = starts[:, None] + jnp.arange(CMP_BLOCK)[None, :]

    def compress(t, pos_emb, w1, w2):
        blocks = kv(t)[:, :, win_idx] + pos_emb
        return jax.nn.gelu(blocks.reshape(b, g_n, n_cmp, CMP_BLOCK * dh) @ w1) @ w2

    k_c = compress(k_cmp_raw, cmp_pos_k, cmp_w1_k, cmp_w2_k)
    v_c = compress(v_cmp_raw, cmp_pos_v, cmp_w1_v, cmp_w2_v)
    dist_c = t_pos[:, None] - (starts + CMP_BLOCK - 1)[None, :]
    valid_c = dist_c >= 0
    s_c = jnp.einsum("bgrtd,bgjd->bgrtj", q, k_c) + tbl[:, :, rel_bucket(dist_c)]
    p_c = jax.nn.softmax(jnp.where(valid_c, s_c, NEG_INF), axis=-1) * valid_c
    o_c = jnp.einsum("bgrtj,bgjd->bgrtd", p_c, v_c)

    n_sel = s // SEL_BLOCK
    top = min(SEL_TOPK, n_sel)
    sel_start = jnp.arange(n_sel) * SEL_BLOCK
    overlap = ((starts[:, None] < sel_start[None, :] + SEL_BLOCK)
               & (starts[:, None] + CMP_BLOCK > sel_start[None, :])).astype(f32)
    imp = jnp.einsum("bgrtj,js->bgts", p_c, overlap)
    blk = jnp.arange(n_sel)[None, :]
    cur = (t_pos // SEL_BLOCK)[:, None]
    forced = (blk == 0) | (blk == cur) | (blk == cur - 1)
    imp = jnp.where(forced, jnp.inf, jnp.where(blk > cur, -jnp.inf, imp))
    _, sel_idx = lax.top_k(imp, top)

    k_sb = kv(k_slc).reshape(b, g_n, n_sel, SEL_BLOCK, dh)
    v_sb = kv(v_slc).reshape(b, g_n, n_sel, SEL_BLOCK, dh)
    n_qb = s // SEL_Q_BLOCK
    q_ch = q.reshape(b, g_n, r_n, n_qb, SEL_Q_BLOCK, dh).transpose(3, 0, 1, 2, 4, 5)
    idx_ch = sel_idx.reshape(b, g_n, n_qb, SEL_Q_BLOCK, top).transpose(2, 0, 1, 3, 4)
    t_ch = t_pos.reshape(n_qb, SEL_Q_BLOCK)
    b_ix = jnp.arange(b)[:, None, None, None]
    g_ix = jnp.arange(g_n)[:, None, None]
    g_ix5 = jnp.arange(g_n)[None, :, None, None, None]
    r_ix5 = jnp.arange(r_n)[None, None, :, None, None]
    n_keys = top * SEL_BLOCK

    def sel_block(args):
        qb, ib, tb = args
        kb = k_sb[b_ix, g_ix, ib].reshape(b, g_n, SEL_Q_BLOCK, n_keys, dh)
        vb = v_sb[b_ix, g_ix, ib].reshape(b, g_n, SEL_Q_BLOCK, n_keys, dh)
        key_pos = (ib[..., None] * SEL_BLOCK + jnp.arange(SEL_BLOCK)).reshape(b, g_n, SEL_Q_BLOCK, n_keys)
        dist = tb[:, None] - key_pos
        bias = tbl[g_ix5, r_ix5, rel_bucket(dist)[:, :, None]]
        sc = jnp.einsum("bgrqd,bgqkd->bgrqk", qb, kb) + bias
        p = jax.nn.softmax(jnp.where(dist[:, :, None] >= 0, sc, NEG_INF), axis=-1)
        return jnp.einsum("bgrqk,bgqkd->bgrqd", p, vb)

    o_s = lax.map(sel_block, (q_ch, idx_ch, t_ch))
    o_s = o_s.transpose(1, 2, 3, 0, 4, 5).reshape(b, g_n, r_n, s, dh)

    nb = s // WIN_Q_BLOCK
    n_back = WINDOW // WIN_Q_BLOCK

    def band(t):
        tp = jnp.pad(kv(t), ((0, 0), (0, 0), (WINDOW, 0), (0, 0))).reshape(b, g_n, nb + n_back, WIN_Q_BLOCK, dh)
        return jnp.concatenate([tp[:, :, i:i + nb] for i in range(n_back + 1)], axis=3)

    k_wb, v_wb = band(k_win), band(v_win)
    qi = jnp.arange(WIN_Q_BLOCK)
    kj = jnp.arange(WINDOW + WIN_Q_BLOCK)
    dist_w = WINDOW + qi[:, None] - kj[None, :]
    key_pos_w = (jnp.arange(nb)[:, None] - n_back) * WIN_Q_BLOCK + kj[None, :]
    valid_w = ((dist_w >= 0) & (dist_w < WINDOW))[None] & (key_pos_w >= 0)[:, None, :]
    q_wb = q.reshape(b, g_n, r_n, nb, WIN_Q_BLOCK, dh)
    s_w = jnp.einsum("bgrnqd,bgnkd->bgrnqk", q_wb, k_wb) + tbl[:, :, rel_bucket(dist_w)][:, :, None]
    p_w = jax.nn.softmax(jnp.where(valid_w, s_w, NEG_INF), axis=-1)
    o_w = jnp.einsum("bgrnqk,bgnkd->bgrnqd", p_w, v_wb).reshape(b, g_n, r_n, s, dh)

    gt = jax.nn.sigmoid(gate_pre.astype(f32)).reshape(b, s, 3, g_n, r_n).transpose(2, 0, 3, 4, 1)[..., None]
    o = gt[0] * o_c + gt[1] * o_s + gt[2] * o_w
    return o.transpose(0, 3, 1, 2, 4).reshape(b, s, NSA_WIDTH)


def token_mixer(h, rel_bias, w_in, b_in, conv_qk, mlstm_norm_g, cmp_pos_k, cmp_w1_k, cmp_w2_k,
                cmp_pos_v, cmp_w1_v, cmp_w2_v, ret_norm_g, w_br_mlstm, w_br_nsa, w_br_ret,
                w_gate_down, w_gate_up, b_gate, w_out):
    b, s, _ = h.shape
    f32 = jnp.float32
    (m_q, m_k, m_v, m_o, m_i, m_f, n_q, n_kc, n_vc, n_ks, n_vs, n_kw, n_vw, n_gate,
     r_q, r_k, r_v, r_g) = split_in_proj(h @ w_in + b_in)

    def to_heads(t, n_heads):
        return t.astype(f32).reshape(b, s, n_heads, HEAD_DIM).transpose(0, 2, 1, 3)

    qk = jax.nn.silu(causal_depthwise_conv(jnp.concatenate([m_q, m_k], axis=-1), conv_qk))
    m_q, m_k = jnp.split(qk, 2, axis=-1)
    h_a = mlstm_chunkwise(to_heads(m_q, MLSTM_HEADS), to_heads(m_k, MLSTM_HEADS), to_heads(m_v, MLSTM_HEADS),
                          m_i.astype(f32).transpose(0, 2, 1), m_f.astype(f32).transpose(0, 2, 1))
    y_a = jax.nn.sigmoid(m_o.astype(f32)) * head_norm(h_a.transpose(0, 2, 1, 3), mlstm_norm_g)

    y_b = nsa_attention(n_q, n_kc, n_vc, n_ks, n_vs, n_kw, n_vw, n_gate, rel_bias,
                        cmp_pos_k, cmp_w1_k, cmp_w2_k, cmp_pos_v, cmp_w1_v, cmp_w2_v)

    pos = jnp.arange(s)
    h_c = retention_chunkwise(rope(to_heads(r_q, RET_HEADS), pos), rope(to_heads(r_k, RET_HEADS), pos),
                              to_heads(r_v, RET_HEADS))
    y_c = jax.nn.silu(r_g.astype(f32)) * head_norm(h_c.transpose(0, 2, 1, 3), ret_norm_g)

    g_low = h @ w_gate_down

    def gate(i):
        lo, hi = i * D_MODEL, (i + 1) * D_MODEL
        return jax.nn.sigmoid((g_low @ w_gate_up[:, lo:hi] + b_gate[lo:hi]).astype(f32))

    merged = gate(0) * (y_a @ w_br_mlstm) + gate(1) * (y_b @ w_br_nsa) + gate(2) * (y_c @ w_br_ret)
    return (merged @ w_out).astype(h.dtype)


def squared_relu_mlp(h, w_up, w_down):
    return (jnp.square(jax.nn.relu(h @ w_up)) @ w_down).astype(h.dtype)


def setup_inputs(seed: int = 0) -> dict:
    key = jax.random.key(seed)
    ks = jax.random.split(key, 32)

    def nrm(k, shape, scale):
        return jax.random.normal(k, shape, jnp.float32) * scale

    f_off = 4 * MLSTM_WIDTH + MLSTM_HEADS
    b_in = nrm(ks[4], (DEPTH, IN_COLS), 0.02)
    b_in = b_in.at[:, f_off:f_off + MLSTM_HEADS].add(jnp.linspace(3.0, 6.0, MLSTM_HEADS))
    cmp_in = CMP_BLOCK * HEAD_DIM
    return {
        "x": nrm(ks[0], (BATCH, SEQ, D_MODEL), 1.0),
        "rel_bias": nrm(ks[1], (REL_BUCKETS, NSA_HEADS), 0.5),
        "norm_mix_g": 1.0 + nrm(ks[2], (DEPTH, D_MODEL), 0.02),
        "w_in": nrm(ks[3], (DEPTH, D_MODEL, IN_COLS), D_MODEL ** -0.5),
        "b_in": b_in,
        "conv_qk": nrm(ks[5], (DEPTH, CONV_WIDTH, 2 * MLSTM_WIDTH), CONV_WIDTH ** -0.5),
        "mlstm_norm_g": 1.0 + nrm(ks[6], (DEPTH, MLSTM_WIDTH), 0.02),
        "cmp_pos_k": nrm(ks[7], (DEPTH, CMP_BLOCK, HEAD_DIM), 0.02),
        "cmp_w1_k": nrm(ks[8], (DEPTH, cmp_in, HEAD_DIM), cmp_in ** -0.5),
        "cmp_w2_k": nrm(ks[9], (DEPTH, HEAD_DIM, HEAD_DIM), HEAD_DIM ** -0.5),
        "cmp_pos_v": nrm(ks[10], (DEPTH, CMP_BLOCK, HEAD_DIM), 0.02),
        "cmp_w1_v": nrm(ks[11], (DEPTH, cmp_in, HEAD_DIM), cmp_in ** -0.5),
        "cmp_w2_v": nrm(ks[12], (DEPTH, HEAD_DIM, HEAD_DIM), HEAD_DIM ** -0.5),
        "ret_norm_g": 1.0 + nrm(ks[13], (DEPTH, RET_WIDTH), 0.02),
        "w_br_mlstm": nrm(ks[14], (DEPTH, MLSTM_WIDTH, D_MODEL), MLSTM_WIDTH ** -0.5),
        "w_br_nsa": nrm(ks[15], (DEPTH, NSA_WIDTH, D_MODEL), NSA_WIDTH ** -0.5),
        "w_br_ret": nrm(ks[16], (DEPTH, RET_WIDTH, D_MODEL), RET_WIDTH ** -0.5),
        "w_gate_down": nrm(ks[17], (DEPTH, D_MODEL, GATE_RANK), D_MODEL ** -0.5),
        "w_gate_up": nrm(ks[18], (DEPTH, GATE_RANK, N_BRANCH * D_MODEL), GATE_RANK ** -0.5),
        "b_gate": nrm(ks[19], (DEPTH, N_BRANCH * D_MODEL), 0.02),
        "w_out": nrm(ks[20], (DEPTH, D_MODEL, D_MODEL), D_MODEL ** -0.5),
        "norm_mlp_g": 1.0 + nrm(ks[21], (DEPTH, D_MODEL), 0.02),
        "w_up": nrm(ks[22], (DEPTH, D_MODEL, D_FF), D_MODEL ** -0.5),
        "w_down": nrm(ks[23], (DEPTH, D_FF, D_MODEL), D_FF ** -0.5),
        "final_norm_g": 1.0 + nrm(ks[24], (D_MODEL,), 0.02),
    }


def reference(x, rel_bias, norm_mix_g, w_in, b_in, conv_qk, mlstm_norm_g, cmp_pos_k, cmp_w1_k, cmp_w2_k,
              cmp_pos_v, cmp_w1_v, cmp_w2_v, ret_norm_g, w_br_mlstm, w_br_nsa, w_br_ret,
              w_gate_down, w_gate_up, b_gate, w_out, norm_mlp_g, w_up, w_down, final_norm_g):
    for l in range(DEPTH):
        h = rms_norm(x, norm_mix_g[l])
        x = x + token_mixer(h, rel_bias, w_in[l], b_in[l], conv_qk[l], mlstm_norm_g[l],
                            cmp_pos_k[l], cmp_w1_k[l], cmp_w2_k[l], cmp_pos_v[l], cmp_w1_v[l], cmp_w2_v[l],
                            ret_norm_g[l], w_br_mlstm[l], w_br_nsa[l], w_br_ret[l],
                            w_gate_down[l], w_gate_up[l], b_gate[l], w_out[l])
        h = rms_norm(x, norm_mlp_g[l])
        x = x + squared_relu_mlp(h, w_up[l], w_down[l])
    return rms_norm(x, final_norm_g)
```

```python
import functools
import math

import numpy as np
import jax
import jax.numpy as jnp
from jax import lax
from jax.experimental import pallas as pl
from jax.experimental.pallas import tpu as pltpu

D_MODEL = 4096
HEAD_DIM = 128
MIX_WIDTH = D_MODEL // 2
MLSTM_HEADS = MIX_WIDTH // (4 * HEAD_DIM)
NSA_HEADS = MIX_WIDTH // (2 * HEAD_DIM)
RET_HEADS = MIX_WIDTH // (4 * HEAD_DIM)
NSA_KV_GROUPS = 2
NSA_GROUP_SIZE = NSA_HEADS // NSA_KV_GROUPS
MLSTM_WIDTH = MLSTM_HEADS * HEAD_DIM
NSA_WIDTH = NSA_HEADS * HEAD_DIM
RET_WIDTH = RET_HEADS * HEAD_DIM
NSA_KV_WIDTH = NSA_KV_GROUPS * HEAD_DIM
CONV_WIDTH = 4
CMP_BLOCK = 32
CMP_STRIDE = 16
SEL_BLOCK = 64
SEL_TOPK = 8
WINDOW = 512
REL_BUCKETS = 32
REL_MAX_DIST = 128
GATE_RANK = D_MODEL // 4
N_BRANCH = 3
EPS = 1e-6
NEG_INF = -1e30

CHUNK = 128
TILE = 128

OFF_GL = 0
OFF_MQ = OFF_GL + GATE_RANK
OFF_MK = OFF_MQ + MLSTM_WIDTH
OFF_MV = OFF_MK + MLSTM_WIDTH
OFF_MO = OFF_MV + MLSTM_WIDTH
OFF_NQ = OFF_MO + MLSTM_WIDTH
OFF_NKC = OFF_NQ + NSA_WIDTH
OFF_NVC = OFF_NKC + NSA_KV_WIDTH
OFF_NKS = OFF_NVC + NSA_KV_WIDTH
OFF_NVS = OFF_NKS + NSA_KV_WIDTH
OFF_NKW = OFF_NVS + NSA_KV_WIDTH
OFF_NVW = OFF_NKW + NSA_KV_WIDTH
OFF_RQ = OFF_NVW + NSA_KV_WIDTH
OFF_RK = OFF_RQ + RET_WIDTH
OFF_RV = OFF_RK + RET_WIDTH
OFF_RG = OFF_RV + RET_WIDTH
MAIN_COLS = OFF_RG + RET_WIDTH
SMALL_COLS = 128
SM_I = 0
SM_F = MLSTM_HEADS
SM_GATE = 2 * MLSTM_HEADS

VMEM_LIMIT = 56 * 1024 * 1024

_NT = (((1,), (1,)), ((), ()))


def _sigmoid(x):
    return 1.0 / (1.0 + jnp.exp(-x))


def _cparams(sem):
    return pltpu.CompilerParams(dimension_semantics=sem, vmem_limit_bytes=VMEM_LIMIT)


def _rmsnorm_kernel(x_ref, g_ref, o_ref):
    x = x_ref[...]
    ms = jnp.mean(x * x, axis=-1, keepdims=True)
    o_ref[...] = (x * lax.rsqrt(ms + EPS) * g_ref[...]).astype(o_ref.dtype)


def _rmsnorm(x, gain, out_dtype, tm=256):
    t, d = x.shape
    return pl.pallas_call(
        _rmsnorm_kernel,
        grid=(t // tm,),
        in_specs=[pl.BlockSpec((tm, d), lambda i: (i, 0)),
                  pl.BlockSpec((1, d), lambda i: (0, 0))],
        out_specs=pl.BlockSpec((tm, d), lambda i: (i, 0)),
        out_shape=jax.ShapeDtypeStruct((t, d), out_dtype),
        compiler_params=_cparams(("parallel",)),
        name="rmsnorm",
    )(x, gain.reshape(1, d))


def _mm_kernel(*refs, nk, act, has_bias, has_res):
    a_ref, w_ref = refs[0], refs[1]
    pos = 2
    b_ref = refs[pos] if has_bias else None
    pos += int(has_bias)
    r_ref = refs[pos] if has_res else None
    pos += int(has_res)
    o_ref = refs[pos]
    acc_ref = refs[pos + 1] if nk > 1 else None

    part = jnp.dot(a_ref[...], w_ref[...], preferred_element_type=jnp.float32)

    def finish(acc):
        if has_bias:
            acc = acc + b_ref[...]
        if act == "relu2":
            acc = jnp.square(jnp.maximum(acc, 0.0))
        if has_res:
            acc = acc + r_ref[...]
        o_ref[...] = acc.astype(o_ref.dtype)

    if nk == 1:
        finish(part)
    else:
        k = pl.program_id(2)

        @pl.when(k == 0)
        def _():
            acc_ref[...] = part

        @pl.when(k > 0)
        def _():
            acc_ref[...] += part

        @pl.when(k == nk - 1)
        def _():
            finish(acc_ref[...])


def _matmul(a, w, *, bias=None, res=None, act=None, out_dtype, tm, tn, tk, name):
    m, kdim = a.shape
    n = w.shape[1]
    tm, tn, tk = min(tm, m), min(tn, n), min(tk, kdim)
    nk = kdim // tk
    in_specs = [pl.BlockSpec((tm, tk), lambda j, i, k: (i, k)),
                pl.BlockSpec((tk, tn), lambda j, i, k: (k, j))]
    args = [a, w]
    if bias is not None:
        in_specs.append(pl.BlockSpec((1, tn), lambda j, i, k: (0, j)))
        args.append(bias.reshape(1, n))
    if res is not None:
        in_specs.append(pl.BlockSpec((tm, tn), lambda j, i, k: (i, j)))
        args.append(res)
    scratch = [pltpu.VMEM((tm, tn), jnp.float32)] if nk > 1 else []
    return pl.pallas_call(
        functools.partial(_mm_kernel, nk=nk, act=act, has_bias=bias is not None, has_res=res is not None),
        grid=(n // tn, m // tm, nk),
        in_specs=in_specs,
        out_specs=pl.BlockSpec((tm, tn), lambda j, i, k: (i, j)),
        out_shape=jax.ShapeDtypeStruct((m, n), out_dtype),
        scratch_shapes=scratch,
        compiler_params=_cparams(("parallel", "parallel", "arbitrary")),
        name=name,
    )(*args)


def _merge_kernel(gl_ref, ya_ref, yb_ref, yc_ref, wg0_ref, wg1_ref, wg2_ref, bg0_ref, bg1_ref, bg2_ref,
                  wa_ref, wb_ref, wc_ref, o_ref):
    gl = gl_ref[...].astype(jnp.bfloat16)

    def branch(wg_ref, bg_ref, y_ref, w_ref):
        gate = _sigmoid(jnp.dot(gl, wg_ref[...], preferred_element_type=jnp.float32) + bg_ref[...])
        return gate * jnp.dot(y_ref[...], w_ref[...], preferred_element_type=jnp.float32)

    out = (branch(wg0_ref, bg0_ref, ya_ref, wa_ref) + branch(wg1_ref, bg1_ref, yb_ref, wb_ref)
           + branch(wg2_ref, bg2_ref, yc_ref, wc_ref))
    o_ref[...] = out.astype(o_ref.dtype)


def _merge(z, ya, yb, yc, w_gate_up, b_gate, wa, wb, wc, tm=512, tn=512):
    t = z.shape[0]
    d = wa.shape[1]
    tm = min(tm, t)
    nj = d // tn
    assert OFF_GL % GATE_RANK == 0
    gl_blk = OFF_GL // GATE_RANK

    def wg_spec(i):
        return pl.BlockSpec((GATE_RANK, tn), lambda j, m, i=i: (0, i * nj + j))

    def bg_spec(i):
        return pl.BlockSpec((1, tn), lambda j, m, i=i: (0, i * nj + j))

    def row_spec(width):
        return pl.BlockSpec((tm, width), lambda j, m: (m, 0))

    def w_spec(width):
        return pl.BlockSpec((width, tn), lambda j, m: (0, j))

    bg = b_gate.reshape(1, N_BRANCH * d)
    return pl.pallas_call(
        _merge_kernel,
        grid=(nj, t // tm),
        in_specs=[pl.BlockSpec((tm, GATE_RANK), lambda j, m: (m, gl_blk)),
                  row_spec(MLSTM_WIDTH), row_spec(NSA_WIDTH), row_spec(RET_WIDTH),
                  wg_spec(0), wg_spec(1), wg_spec(2), bg_spec(0), bg_spec(1), bg_spec(2),
                  w_spec(MLSTM_WIDTH), w_spec(NSA_WIDTH), w_spec(RET_WIDTH)],
        out_specs=pl.BlockSpec((tm, tn), lambda j, m: (m, j)),
        out_shape=jax.ShapeDtypeStruct((t, d), jnp.bfloat16),
        compiler_params=_cparams(("parallel", "parallel")),
        name="gated_merge",
    )(z, ya, yb, yc, w_gate_up, w_gate_up, w_gate_up, bg, bg, bg, wa, wb, wc)


def _head_norm(h, gain):
    mu = jnp.mean(h, axis=-1, keepdims=True)
    d = h - mu
    var = jnp.mean(d * d, axis=-1, keepdims=True)
    return d * lax.rsqrt(var + EPS) * gain


def _lane_pick(x, lane_ids, lane):
    return jnp.sum(jnp.where(lane_ids == lane, x, 0.0), axis=1, keepdims=True)


def _mlstm_kernel(q_ref, k_ref, v_ref, og_ref, sm_ref, cq_ref, ck_ref, gain_ref, o_ref, *, n_chunks):
    L = CHUNK
    h = pl.program_id(1)
    lane_ids = lax.broadcasted_iota(jnp.int32, (L, SMALL_COLS), 1)
    row = lax.broadcasted_iota(jnp.int32, (L, L), 0)
    col = lax.broadcasted_iota(jnp.int32, (L, L), 1)
    causal = col <= row
    eye = col == row
    cq = cq_ref[...]
    ck = ck_ref[...]
    gain = gain_ref[...]
    scale = HEAD_DIM ** -0.5

    def conv_silu(ref, w, c, start):
        cur = ref[pl.ds(start, L), :]
        pstart = pl.multiple_of(jnp.maximum(start - 8, 0), 8)
        prev = jnp.where(c > 0, ref[pl.ds(pstart, 8), :], 0.0)
        ext = jnp.concatenate([prev, cur], axis=0)
        y = w[CONV_WIDTH - 1:CONV_WIDTH, :] * cur
        for kk in range(CONV_WIDTH - 1):
            off = 8 - (CONV_WIDTH - 1) + kk
            y = y + w[kk:kk + 1, :] * ext[off:off + L, :]
        return y * _sigmoid(y)

    def body(c, carry):
        cmat, nvec, m = carry
        start = pl.multiple_of(c * L, L)
        q = conv_silu(q_ref, cq, c, start)
        k = conv_silu(k_ref, ck, c, start) * scale
        v = v_ref[pl.ds(start, L), :]
        sm = sm_ref[pl.ds(start, L), :]
        i_col = _lane_pick(sm, lane_ids, SM_I + h)
        f_col = _lane_pick(sm, lane_ids, SM_F + h)
        ls_col = jnp.minimum(f_col, 0.0) - jnp.log(1.0 + jnp.exp(-jnp.abs(f_col)))
        a_row = jnp.sum(jnp.where(row <= col, ls_col, 0.0), axis=0, keepdims=True)
        a_col = jnp.sum(jnp.where(eye, a_row, 0.0), axis=1, keepdims=True)
        i_row = jnp.sum(jnp.where(eye, i_col, 0.0), axis=0, keepdims=True)
        g = jnp.sum(ls_col, axis=0, keepdims=True)
        w_col = g - a_col + i_col
        m_loc = jnp.max(w_col, axis=0, keepdims=True)
        e_col = jnp.exp(w_col - m_loc)

        log_d = jnp.where(causal, a_col - a_row + i_row, NEG_INF)
        log_inter = a_col + m
        m_row = jnp.maximum(log_inter, jnp.max(log_d, axis=1, keepdims=True))
        inter = jnp.exp(log_inter - m_row)
        dmat = jnp.exp(log_d - m_row)

        qb = q.astype(jnp.bfloat16)
        kb = k.astype(jnp.bfloat16)
        vb = v.astype(jnp.bfloat16)
        sc = lax.dot_general(qb, kb, _NT, preferred_element_type=jnp.float32) * dmat
        num = (inter * jnp.dot(qb, cmat.astype(jnp.bfloat16), preferred_element_type=jnp.float32)
               + jnp.dot(sc.astype(jnp.bfloat16), vb, preferred_element_type=jnp.float32))
        den = inter * jnp.sum(q * nvec, axis=1, keepdims=True) + jnp.sum(sc, axis=1, keepdims=True)
        hout = num / jnp.maximum(jnp.abs(den), jnp.exp(-m_row))

        y = _head_norm(hout, gain)
        o_ref[pl.ds(start, L), :] = (_sigmoid(og_ref[pl.ds(start, L), :]) * y).astype(o_ref.dtype)

        m_new = jnp.maximum(g + m, m_loc)
        a_old = jnp.exp(g + m - m_new)
        a_new = jnp.exp(m_loc - m_new)
        ke = k * e_col
        c_new = a_old * cmat + a_new * jnp.dot(ke.T.astype(jnp.bfloat16), vb, preferred_element_type=jnp.float32)
        n_new = a_old * nvec + a_new * jnp.sum(ke, axis=0, keepdims=True)
        return c_new, n_new, m_new

    init = (jnp.zeros((HEAD_DIM, HEAD_DIM), jnp.float32), jnp.zeros((1, HEAD_DIM), jnp.float32),
            jnp.zeros((1, 1), jnp.float32))
    lax.fori_loop(0, n_chunks, body, init)


def _mlstm(z, zs, conv_qk, gain, batch, seq):
    nh = MLSTM_HEADS
    hb = HEAD_DIM

    def seg(off):
        return pl.BlockSpec((seq, hb), lambda b, h, off=off: (b, off // hb + h))

    return pl.pallas_call(
        functools.partial(_mlstm_kernel, n_chunks=seq // CHUNK),
        grid=(batch, nh),
        in_specs=[seg(OFF_MQ), seg(OFF_MK), seg(OFF_MV), seg(OFF_MO),
                  pl.BlockSpec((seq, SMALL_COLS), lambda b, h: (b, 0)),
                  pl.BlockSpec((CONV_WIDTH, hb), lambda b, h: (0, h)),
                  pl.BlockSpec((CONV_WIDTH, hb), lambda b, h: (0, nh + h)),
                  pl.BlockSpec((1, hb), lambda b, h: (0, h))],
        out_specs=pl.BlockSpec((seq, hb), lambda b, h: (b, h)),
        out_shape=jax.ShapeDtypeStruct((batch * seq, MLSTM_WIDTH), jnp.bfloat16),
        compiler_params=_cparams(("parallel", "parallel")),
        name="mlstm",
    )(z, z, z, z, zs, conv_qk, conv_qk, gain.reshape(1, MLSTM_WIDTH))


def _retention_kernel(q_ref, k_ref, v_ref, g_ref, cos_ref, sin_ref, lg_ref, gain_ref, o_ref, *, n_chunks):
    L = CHUNK
    row = lax.broadcasted_iota(jnp.int32, (L, L), 0)
    col = lax.broadcasted_iota(jnp.int32, (L, L), 1)
    rel = (row - col).astype(jnp.float32)
    lg = lg_ref[0][:, 0:1]
    decay = jnp.where(rel >= 0, jnp.exp(lg * jnp.maximum(rel, 0.0)), 0.0)
    idx = lax.broadcasted_iota(jnp.int32, (L, 1), 0).astype(jnp.float32)
    xi = jnp.exp(lg * (idx + 1.0))
    zeta = jnp.exp(lg * (L - 1.0 - idx))
    chunk_decay = jnp.exp(lg * L)
    gain = gain_ref[...]
    scale = HEAD_DIM ** -0.5

    def body(c, rmat):
        start = pl.multiple_of(c * L, L)
        cosf = cos_ref[pl.ds(start, L), :]
        sinf = sin_ref[pl.ds(start, L), :]

        def rot(ref):
            x = ref[pl.ds(start, L), :]
            return x * cosf + pltpu.roll(x, HEAD_DIM // 2, 1) * sinf

        q = rot(q_ref)
        k = rot(k_ref) * scale
        vb = v_ref[pl.ds(start, L), :].astype(jnp.bfloat16)
        qb = q.astype(jnp.bfloat16)
        inner = lax.dot_general(qb, k.astype(jnp.bfloat16), _NT, preferred_element_type=jnp.float32) * decay
        out = (jnp.dot(inner.astype(jnp.bfloat16), vb, preferred_element_type=jnp.float32)
               + xi * jnp.dot(qb, rmat.astype(jnp.bfloat16), preferred_element_type=jnp.float32))
        y = _head_norm(out, gain)
        gate = g_ref[pl.ds(start, L), :]
        o_ref[pl.ds(start, L), :] = (gate * _sigmoid(gate) * y).astype(o_ref.dtype)
        kz = (k * zeta).T.astype(jnp.bfloat16)
        return chunk_decay * rmat + jnp.dot(kz, vb, preferred_element_type=jnp.float32)

    lax.fori_loop(0, n_chunks, body, jnp.zeros((HEAD_DIM, HEAD_DIM), jnp.float32))


def _retention(z, cosf, sinf, lg, gain, batch, seq):
    nh = RET_HEADS
    hb = HEAD_DIM

    def seg(off):
        return pl.BlockSpec((seq, hb), lambda b, h, off=off: (b, off // hb + h))

    table = pl.BlockSpec((seq, hb), lambda b, h: (0, 0))
    return pl.pallas_call(
        functools.partial(_retention_kernel, n_chunks=seq // CHUNK),
        grid=(batch, nh),
        in_specs=[seg(OFF_RQ), seg(OFF_RK), seg(OFF_RV), seg(OFF_RG), table, table,
                  pl.BlockSpec((1, 1, hb), lambda b, h: (h, 0, 0)),
                  pl.BlockSpec((1, hb), lambda b, h: (0, h))],
        out_specs=pl.BlockSpec((seq, hb), lambda b, h: (b, h)),
        out_shape=jax.ShapeDtypeStruct((batch * seq, RET_WIDTH), jnp.bfloat16),
        compiler_params=_cparams(("parallel", "parallel")),
        name="retention",
    )(z, z, z, z, cosf, sinf, lg, gain.reshape(1, RET_WIDTH))


def _gelu_tanh(x):
    return 0.5 * x * (1.0 + jnp.tanh(math.sqrt(2.0 / math.pi) * (x + 0.044715 * (x * x * x))))


def _compress_kernel(k_ref, v_ref, pk_ref, w1k_ref, w2k_ref, pv_ref, w1v_ref, w2v_ref, ko_ref, vo_ref, pad_ref,
                     *, seq):
    nblk = seq // CMP_STRIDE
    for x_ref, p_ref, w1_ref, w2_ref, o_ref in ((k_ref, pk_ref, w1k_ref, w2k_ref, ko_ref),
                                                 (v_ref, pv_ref, w1v_ref, w2v_ref, vo_ref)):
        pad_ref[0:seq, :] = x_ref[...]
        pad_ref[seq:seq + CMP_BLOCK, :] = jnp.zeros((CMP_BLOCK, HEAD_DIM), jnp.float32)
        acc = jnp.zeros((nblk, HEAD_DIM), jnp.float32)
        for p in range(CMP_BLOCK):
            xs = pad_ref[pl.ds(p, nblk, stride=CMP_STRIDE), :] + p_ref[p:p + 1, :]
            acc = acc + jnp.dot(xs.astype(jnp.bfloat16), w1_ref[p * HEAD_DIM:(p + 1) * HEAD_DIM, :],
                                preferred_element_type=jnp.float32)
        mid = _gelu_tanh(acc).astype(jnp.bfloat16)
        o_ref[0, 0] = jnp.dot(mid, w2_ref[...], preferred_element_type=jnp.float32).astype(o_ref.dtype)


def _compress(z, pos_k, w1_k, w2_k, pos_v, w1_v, w2_v, batch, seq):
    hb = HEAD_DIM
    nblk = seq // CMP_STRIDE
    cmp_in = CMP_BLOCK * hb

    def seg(off):
        return pl.BlockSpec((seq, hb), lambda b, g, off=off: (b, off // hb + g))

    def full(shape):
        return pl.BlockSpec(shape, lambda b, g: (0,) * len(shape))

    out_spec = pl.BlockSpec((1, 1, nblk, hb), lambda b, g: (b, g, 0, 0))
    out_shape = jax.ShapeDtypeStruct((batch, NSA_KV_GROUPS, nblk, hb), jnp.bfloat16)
    return pl.pallas_call(
        functools.partial(_compress_kernel, seq=seq),
        grid=(batch, NSA_KV_GROUPS),
        in_specs=[seg(OFF_NKC), seg(OFF_NVC),
                  full((CMP_BLOCK, hb)), full((cmp_in, hb)), full((hb, hb)),
                  full((CMP_BLOCK, hb)), full((cmp_in, hb)), full((hb, hb))],
        out_specs=[out_spec, out_spec],
        out_shape=[out_shape, out_shape],
        scratch_shapes=[pltpu.VMEM((seq + CMP_BLOCK, hb), jnp.float32)],
        compiler_params=_cparams(("parallel", "parallel")),
        name="nsa_compress",
    )(z, z, pos_k, w1_k, w2_k, pos_v, w1_v, w2_v)


def _rel_bucket_np(dist):
    n = np.maximum(dist, 0)
    exact = REL_BUCKETS // 2
    log_ratio = (np.log(np.maximum(n, 1).astype(np.float32) / np.float32(exact))
                 / np.float32(math.log(REL_MAX_DIST / exact)))
    large = np.minimum(exact + (log_ratio * np.float32(REL_BUCKETS - exact)).astype(np.int32), REL_BUCKETS - 1)
    return np.where(n < exact, n, large).astype(np.int32)


def _bias_kernel(tbl_ref, idxc_ref, idxt_ref, bc_ref, bt_ref):
    h = pl.program_id(0)
    idxc = idxc_ref[...]
    idxt = idxt_ref[...]
    accc = jnp.zeros(idxc.shape, jnp.float32)
    acct = jnp.zeros(idxt.shape, jnp.float32)
    for b in range(REL_BUCKETS):
        val = tbl_ref[b, h]
        accc = jnp.where(idxc == b, val, accc)
        acct = jnp.where(idxt == b, val, acct)
    bc_ref[0] = accc
    bt_ref[:, 0] = acct


def _bias_tables(rel_bias, seq):
    nblk = seq // CMP_STRIDE
    t = np.arange(seq)[:, None]
    j = np.arange(nblk)[None, :]
    idx_c = _rel_bucket_np(t - (j * CMP_STRIDE + CMP_BLOCK - 1))
    i = np.arange(TILE)[:, None]
    kk = np.arange(TILE)[None, :]
    idx_t = np.stack([_rel_bucket_np(i - kk + d * TILE) for d in range(3)])
    return pl.pallas_call(
        _bias_kernel,
        grid=(NSA_HEADS,),
        in_specs=[pl.BlockSpec(memory_space=pltpu.SMEM),
                  pl.BlockSpec((seq, nblk), lambda h: (0, 0)),
                  pl.BlockSpec((3, TILE, TILE), lambda h: (0, 0, 0))],
        out_specs=[pl.BlockSpec((1, seq, nblk), lambda h: (h, 0, 0)),
                   pl.BlockSpec((3, 1, TILE, TILE), lambda h: (0, h, 0, 0))],
        out_shape=[jax.ShapeDtypeStruct((NSA_HEADS, seq, nblk), jnp.float32),
                   jax.ShapeDtypeStruct((3, NSA_HEADS, TILE, TILE), jnp.float32)],
        compiler_params=_cparams(("parallel",)),
        name="nsa_bias_tables",
    )(rel_bias, jnp.asarray(idx_c), jnp.asarray(idx_t))


def _nsa_kernel(q_ref, kc_ref, vc_ref, ks_ref, vs_ref, kw_ref, vw_ref, sm_ref, bc_ref, bt_ref, o_ref,
                acc_ref, m_ref, l_ref, *, seq):
    R = NSA_GROUP_SIZE
    T = TILE
    nblk = seq // CMP_STRIDE
    n_sel = seq // SEL_BLOCK
    top = min(SEL_TOPK, n_sel)
    g = pl.program_id(1)
    n = pl.program_id(2)
    t0 = n * T

    q = q_ref[...] * (HEAD_DIM ** -0.5)
    qs = jnp.concatenate([q[:, r * HEAD_DIM:(r + 1) * HEAD_DIM] for r in range(R)], axis=0).astype(jnp.bfloat16)

    def rep(x):
        return jnp.concatenate([x] * R, axis=0)

    rows_c = lax.broadcasted_iota(jnp.int32, (T, nblk), 0)
    lanes_c = lax.broadcasted_iota(jnp.int32, (T, nblk), 1)
    valid_c = rep((t0 + rows_c) >= lanes_c * CMP_STRIDE + (CMP_BLOCK - 1))
    s_c = (lax.dot_general(qs, kc_ref[0, 0], _NT, preferred_element_type=jnp.float32)
           + bc_ref[...].reshape(R * T, nblk))
    s_c = jnp.where(valid_c, s_c, NEG_INF)
    e_c = jnp.exp(s_c - jnp.max(s_c, axis=1, keepdims=True))
    p_c = jnp.where(valid_c, e_c / jnp.sum(e_c, axis=1, keepdims=True), 0.0)
    o_c = jnp.dot(p_c.astype(jnp.bfloat16), vc_ref[0, 0], preferred_element_type=jnp.float32)

    p_sum = p_c[0:T]
    for r in range(1, R):
        p_sum = p_sum + p_c[r * T:(r + 1) * T]
    ov_j = lax.broadcasted_iota(jnp.int32, (nblk, T), 0)
    ov_s = lax.broadcasted_iota(jnp.int32, (nblk, T), 1)
    ratio = SEL_BLOCK // CMP_STRIDE
    span = CMP_BLOCK // CMP_STRIDE
    overlap = ((ov_j < ratio * ov_s + ratio) & (ov_j + span > ratio * ov_s) & (ov_s < n_sel)).astype(jnp.float32)
    imp = jnp.dot(p_sum, overlap, preferred_element_type=jnp.float32, precision=lax.Precision.HIGHEST)

    rows = lax.broadcasted_iota(jnp.int32, (T, T), 0)
    lanes = lax.broadcasted_iota(jnp.int32, (T, T), 1)
    t_idx = t0 + rows
    cur = t_idx // SEL_BLOCK
    forced = (lanes == 0) | (lanes == cur) | (lanes == cur - 1)
    val = jnp.where(forced, jnp.inf, jnp.where((lanes > cur) | (lanes >= n_sel), -jnp.inf, imp))
    blk_f = lanes.astype(jnp.float32)
    sel = jnp.zeros((T, T), jnp.float32)
    for _ in range(top):
        best = jnp.max(val, axis=1, keepdims=True)
        first = jnp.min(jnp.where(val == best, blk_f, float(T)), axis=1, keepdims=True)
        hit = blk_f == first
        sel = jnp.where(hit, 1.0, sel)
        val = jnp.where(hit, -jnp.inf, val)
    sel_b = sel.astype(jnp.bfloat16)

    m_ref[...] = jnp.full(m_ref.shape, NEG_INF, jnp.float32)
    l_ref[...] = jnp.zeros(l_ref.shape, jnp.float32)
    acc_ref[...] = jnp.zeros(acc_ref.shape, jnp.float32)

    def tile_update(slot, k_ref, v_ref, c, mask):
        ks = pl.multiple_of(c * T, T)
        kb = k_ref[pl.ds(ks, T), :].astype(jnp.bfloat16)
        vb = v_ref[pl.ds(ks, T), :].astype(jnp.bfloat16)
        bias = bt_ref[jnp.minimum(n - c, 2)].reshape(R * T, T)
        s = lax.dot_general(qs, kb, _NT, preferred_element_type=jnp.float32) + bias
        mask = rep(mask)
        m_old = m_ref[slot]
        m_new = jnp.maximum(m_old, jnp.max(jnp.where(mask, s, NEG_INF), axis=1, keepdims=True))
        p = jnp.where(mask, jnp.exp(s - m_new), 0.0)
        alpha = jnp.exp(m_old - m_new)
        l_ref[slot] = alpha * l_ref[slot] + jnp.sum(p, axis=1, keepdims=True)
        acc_ref[slot] = alpha * acc_ref[slot] + jnp.dot(p.astype(jnp.bfloat16), vb, preferred_element_type=jnp.float32)
        m_ref[slot] = m_new

    def sel_body(c, carry):
        kpos = c * T + lanes
        expand = ((c * T + rows) // SEL_BLOCK == lanes).astype(jnp.bfloat16)
        chosen = lax.dot_general(sel_b, expand, _NT, preferred_element_type=jnp.float32) > 0.5
        tile_update(0, ks_ref, vs_ref, c, chosen & (kpos <= t_idx))
        return carry

    lax.fori_loop(0, n + 1, sel_body, 0)

    def win_body(c, carry):
        dist = t_idx - (c * T + lanes)
        tile_update(1, kw_ref, vw_ref, c, (dist >= 0) & (dist < WINDOW))
        return carry

    lax.fori_loop(jnp.maximum(n - WINDOW // T, 0), n + 1, win_body, 0)

    o_s = acc_ref[0] / l_ref[0]
    o_w = acc_ref[1] / l_ref[1]

    lane_ids = lax.broadcasted_iota(jnp.int32, (T, SMALL_COLS), 1)
    sig = _sigmoid(sm_ref[...])
    outs = []
    for r in range(R):
        def gate(branch):
            return _lane_pick(sig, lane_ids, SM_GATE + branch * NSA_HEADS + g * R + r)
        sl = slice(r * T, (r + 1) * T)
        outs.append(gate(0) * o_c[sl] + gate(1) * o_s[sl] + gate(2) * o_w[sl])
    o_ref[...] = jnp.concatenate(outs, axis=1).astype(o_ref.dtype)


def _nsa(z, zs, k_c, v_c, bias_c, bias_t, batch, seq):
    hb = HEAD_DIM
    R = NSA_GROUP_SIZE
    nblk = seq // CMP_STRIDE
    nq = seq // TILE
    gw = R * hb
    assert OFF_NQ % gw == 0

    def seg(off):
        return pl.BlockSpec((seq, hb), lambda b, g, n, off=off: (b, off // hb + g))

    cmp_spec = pl.BlockSpec((1, 1, nblk, hb), lambda b, g, n: (b, g, 0, 0))
    return pl.pallas_call(
        functools.partial(_nsa_kernel, seq=seq),
        grid=(batch, NSA_KV_GROUPS, nq),
        in_specs=[pl.BlockSpec((TILE, gw), lambda b, g, n: (b * nq + n, OFF_NQ // gw + g)),
                  cmp_spec, cmp_spec,
                  seg(OFF_NKS), seg(OFF_NVS), seg(OFF_NKW), seg(OFF_NVW),
                  pl.BlockSpec((TILE, SMALL_COLS), lambda b, g, n: (b * nq + n, 0)),
                  pl.BlockSpec((R, TILE, nblk), lambda b, g, n: (g, n, 0)),
                  pl.BlockSpec((3, R, TILE, TILE), lambda b, g, n: (0, g, 0, 0))],
        out_specs=pl.BlockSpec((TILE, gw), lambda b, g, n: (b * nq + n, g)),
        out_shape=jax.ShapeDtypeStruct((batch * seq, NSA_WIDTH), jnp.bfloat16),
        scratch_shapes=[pltpu.VMEM((2, R * TILE, hb), jnp.float32),
                        pltpu.VMEM((2, R * TILE, 1), jnp.float32),
                        pltpu.VMEM((2, R * TILE, 1), jnp.float32)],
        compiler_params=_cparams(("parallel", "parallel", "arbitrary")),
        name="nsa_attention",
    )(z, k_c, v_c, z, z, z, z, zs, bias_c, bias_t)


def _rope_tables(seq):
    half = HEAD_DIM // 2
    inv_freq = 1.0 / (10000.0 ** jnp.linspace(0.0, 1.0, half))
    ang = jnp.arange(seq).astype(jnp.float32)[:, None] * inv_freq[None, :]
    cos, sin = jnp.cos(ang), jnp.sin(ang)
    return jnp.concatenate([cos, cos], axis=-1), jnp.concatenate([-sin, sin], axis=-1)


def _retention_log_decay():
    lg = np.log1p(-np.exp2(-5.0 - np.arange(RET_HEADS, dtype=np.float64))).astype(np.float32)
    return jnp.asarray(np.broadcast_to(lg[:, None, None], (RET_HEADS, 1, HEAD_DIM)).copy())


def _split_in_proj(w_in, b_in, w_gate_down):
    bf = jnp.bfloat16
    a0 = 4 * MLSTM_WIDTH
    a1 = a0 + 2 * MLSTM_HEADS
    a2 = a1 + NSA_WIDTH + 6 * NSA_KV_WIDTH
    a3 = a2 + 3 * NSA_HEADS
    d = w_in.shape[0]
    w_main = jnp.concatenate([w_gate_down.astype(bf), w_in[:, :a0].astype(bf), w_in[:, a1:a2].astype(bf),
                              w_in[:, a3:].astype(bf)], axis=1)
    b_main = jnp.concatenate([jnp.zeros((GATE_RANK,), jnp.float32), b_in[:a0], b_in[a1:a2], b_in[a3:]])
    n_small = (a1 - a0) + (a3 - a2)
    w_small = jnp.concatenate([w_in[:, a0:a1].astype(bf), w_in[:, a2:a3].astype(bf),
                               jnp.zeros((d, SMALL_COLS - n_small), bf)], axis=1)
    b_small = jnp.concatenate([b_in[a0:a1], b_in[a2:a3], jnp.zeros((SMALL_COLS - n_small,), jnp.float32)])
    return w_main, b_main, w_small, b_small


def _token_mixer(h, x_res, batch, seq, bias_c, bias_t, cosf, sinf, lg, w_in, b_in, conv_qk, mlstm_norm_g,
                 cmp_pos_k, cmp_w1_k, cmp_w2_k, cmp_pos_v, cmp_w1_v, cmp_w2_v, ret_norm_g,
                 w_br_mlstm, w_br_nsa, w_br_ret, w_gate_down, w_gate_up, b_gate, w_out):
    bf = jnp.bfloat16
    w_main, b_main, w_small, b_small = _split_in_proj(w_in, b_in, w_gate_down)
    z = _matmul(h, w_main, bias=b_main, out_dtype=jnp.float32, tm=1024, tn=512, tk=D_MODEL, name="in_proj")
    zs = _matmul(h, w_small, bias=b_small, out_dtype=jnp.float32, tm=1024, tn=SMALL_COLS, tk=D_MODEL,
                 name="in_proj_small")
    y_a = _mlstm(z, zs, conv_qk, mlstm_norm_g, batch, seq)
    k_c, v_c = _compress(z, cmp_pos_k, cmp_w1_k.astype(bf), cmp_w2_k.astype(bf),
                         cmp_pos_v, cmp_w1_v.astype(bf), cmp_w2_v.astype(bf), batch, seq)
    y_b = _nsa(z, zs, k_c, v_c, bias_c, bias_t, batch, seq)
    y_c = _retention(z, cosf, sinf, lg, ret_norm_g, batch, seq)
    merged = _merge(z, y_a, y_b, y_c, w_gate_up.astype(bf), b_gate,
                    w_br_mlstm.astype(bf), w_br_nsa.astype(bf), w_br_ret.astype(bf))
    return _matmul(merged, w_out.astype(bf), res=x_res, out_dtype=jnp.float32, tm=1024, tn=512, tk=D_MODEL,
                   name="out_proj")


def kernel(x, rel_bias, norm_mix_g, w_in, b_in, conv_qk, mlstm_norm_g, cmp_pos_k, cmp_w1_k, cmp_w2_k, cmp_pos_v, cmp_w1_v, cmp_w2_v, ret_norm_g, w_br_mlstm, w_br_nsa, w_br_ret, w_gate_down, w_gate_up, b_gate, w_out, norm_mlp_g, w_up, w_down, final_norm_g):
    batch, seq, d = x.shape
    depth = w_in.shape[0]
    bf = jnp.bfloat16
    bias_c, bias_t = _bias_tables(rel_bias, seq)
    cosf, sinf = _rope_tables(seq)
    lg = _retention_log_decay()
    xr = x.reshape(batch * seq, d)
    for l in range(depth):
        h = _rmsnorm(xr, norm_mix_g[l], bf)
        xr = _token_mixer(h, xr, batch, seq, bias_c, bias_t, cosf, sinf, lg, w_in[l], b_in[l], conv_qk[l],
                          mlstm_norm_g[l], cmp_pos_k[l], cmp_w1_k[l], cmp_w2_k[l], cmp_pos_v[l], cmp_w1_v[l],
                          cmp_w2_v[l], ret_norm_g[l], w_br_mlstm[l], w_br_nsa[l], w_br_ret[l],
                          w_gate_down[l], w_gate_up[l], b_gate[l], w_out[l])
        h = _rmsnorm(xr, norm_mlp_g[l], bf)
        up = _matmul(h, w_up[l].astype(bf), act="relu2", out_dtype=bf, tm=1024, tn=1024, tk=D_MODEL, name="mlp_up")
        xr = _matmul(up, w_down[l].astype(bf), res=xr, out_dtype=jnp.float32, tm=1024, tn=1024, tk=2048,
                     name="mlp_down")
    return _rmsnorm(xr, final_norm_g, jnp.float32).reshape(batch, seq, d)
```

```python
import functools
import math

import numpy as np
import jax
import jax.numpy as jnp
from jax import lax
from jax.experimental import pallas as pl
from jax.experimental.pallas import tpu as pltpu

D_MODEL = 4096
HEAD_DIM = 128
MIX_WIDTH = D_MODEL // 2
MLSTM_HEADS = MIX_WIDTH // (4 * HEAD_DIM)
NSA_HEADS = MIX_WIDTH // (2 * HEAD_DIM)
RET_HEADS = MIX_WIDTH // (4 * HEAD_DIM)
NSA_KV_GROUPS = 2
NSA_GROUP_SIZE = NSA_HEADS // NSA_KV_GROUPS
MLSTM_WIDTH = MLSTM_HEADS * HEAD_DIM
NSA_WIDTH = NSA_HEADS * HEAD_DIM
RET_WIDTH = RET_HEADS * HEAD_DIM
NSA_KV_WIDTH = NSA_KV_GROUPS * HEAD_DIM
CONV_WIDTH = 4
CMP_BLOCK = 32
CMP_STRIDE = 16
SEL_BLOCK = 64
SEL_TOPK = 8
WINDOW = 512
REL_BUCKETS = 32
REL_MAX_DIST = 128
GATE_RANK = D_MODEL // 4
N_BRANCH = 3
EPS = 1e-6
NEG_INF = -1e30
M_FLOOR = -1e28

CHUNK = 128
TILE = 128

OFF_GL = 0
OFF_MQ = OFF_GL + GATE_RANK
OFF_MK = OFF_MQ + MLSTM_WIDTH
OFF_MV = OFF_MK + MLSTM_WIDTH
OFF_MO = OFF_MV + MLSTM_WIDTH
OFF_NQ = OFF_MO + MLSTM_WIDTH
OFF_NKC = OFF_NQ + NSA_WIDTH
OFF_NVC = OFF_NKC + NSA_KV_WIDTH
OFF_NKS = OFF_NVC + NSA_KV_WIDTH
OFF_NVS = OFF_NKS + NSA_KV_WIDTH
OFF_NKW = OFF_NVS + NSA_KV_WIDTH
OFF_NVW = OFF_NKW + NSA_KV_WIDTH
OFF_RQ = OFF_NVW + NSA_KV_WIDTH
OFF_RK = OFF_RQ + RET_WIDTH
OFF_RV = OFF_RK + RET_WIDTH
OFF_RG = OFF_RV + RET_WIDTH
MAIN_COLS = OFF_RG + RET_WIDTH
SMALL_COLS = 128
SM_I = 0
SM_F = MLSTM_HEADS
SM_GATE = 2 * MLSTM_HEADS

VMEM_LIMIT = 56 * 1024 * 1024

_NT = (((1,), (1,)), ((), ()))


def _sigmoid(x):
    return 1.0 / (1.0 + jnp.exp(-x))


def _cparams(sem):
    return pltpu.CompilerParams(dimension_semantics=sem, vmem_limit_bytes=VMEM_LIMIT)


def _rmsnorm_kernel(x_ref, g_ref, o_ref):
    x = x_ref[...]
    ms = jnp.mean(x * x, axis=-1, keepdims=True)
    o_ref[...] = (x * lax.rsqrt(ms + EPS) * g_ref[...]).astype(o_ref.dtype)


def _rmsnorm(x, gain, out_dtype, tm=256):
    t, d = x.shape
    return pl.pallas_call(
        _rmsnorm_kernel,
        grid=(t // tm,),
        in_specs=[pl.BlockSpec((tm, d), lambda i: (i, 0)),
                  pl.BlockSpec((1, d), lambda i: (0, 0))],
        out_specs=pl.BlockSpec((tm, d), lambda i: (i, 0)),
        out_shape=jax.ShapeDtypeStruct((t, d), out_dtype),
        compiler_params=_cparams(("parallel",)),
        name="rmsnorm",
    )(x, gain.reshape(1, d))


def _mm_kernel(*refs, nk, act, has_bias, has_res):
    a_ref, w_ref = refs[0], refs[1]
    pos = 2
    b_ref = refs[pos] if has_bias else None
    pos += int(has_bias)
    r_ref = refs[pos] if has_res else None
    pos += int(has_res)
    o_ref = refs[pos]
    acc_ref = refs[pos + 1] if nk > 1 else None

    part = jnp.dot(a_ref[...], w_ref[...], preferred_element_type=jnp.float32)

    def finish(acc):
        if has_bias:
            acc = acc + b_ref[...]
        if act == "relu2":
            acc = jnp.square(jnp.maximum(acc, 0.0))
        if has_res:
            acc = acc + r_ref[...]
        o_ref[...] = acc.astype(o_ref.dtype)

    if nk == 1:
        finish(part)
    else:
        k = pl.program_id(2)

        @pl.when(k == 0)
        def _():
            acc_ref[...] = part

        @pl.when(k > 0)
        def _():
            acc_ref[...] += part

        @pl.when(k == nk - 1)
        def _():
            finish(acc_ref[...])


def _matmul(a, w, *, bias=None, res=None, act=None, out_dtype, tm, tn, tk, name):
    m, kdim = a.shape
    n = w.shape[1]
    tm, tn, tk = min(tm, m), min(tn, n), min(tk, kdim)
    nk = kdim // tk
    in_specs = [pl.BlockSpec((tm, tk), lambda j, i, k: (i, k)),
                pl.BlockSpec((tk, tn), lambda j, i, k: (k, j))]
    args = [a, w]
    if bias is not None:
        in_specs.append(pl.BlockSpec((1, tn), lambda j, i, k: (0, j)))
        args.append(bias.reshape(1, n))
    if res is not None:
        in_specs.append(pl.BlockSpec((tm, tn), lambda j, i, k: (i, j)))
        args.append(res)
    scratch = [pltpu.VMEM((tm, tn), jnp.float32)] if nk > 1 else []
    return pl.pallas_call(
        functools.partial(_mm_kernel, nk=nk, act=act, has_bias=bias is not None, has_res=res is not None),
        grid=(n // tn, m // tm, nk),
        in_specs=in_specs,
        out_specs=pl.BlockSpec((tm, tn), lambda j, i, k: (i, j)),
        out_shape=jax.ShapeDtypeStruct((m, n), out_dtype),
        scratch_shapes=scratch,
        compiler_params=_cparams(("parallel", "parallel", "arbitrary")),
        name=name,
    )(*args)


def _merge_kernel(gl_ref, ya_ref, yb_ref, yc_ref, wg0_ref, wg1_ref, wg2_ref, bg0_ref, bg1_ref, bg2_ref,
                  wa_ref, wb_ref, wc_ref, o_ref):
    gl = gl_ref[...].astype(jnp.bfloat16)

    def branch(wg_ref, bg_ref, y_ref, w_ref):
        gate = _sigmoid(jnp.dot(gl, wg_ref[...], preferred_element_type=jnp.float32) + bg_ref[...])
        return gate * jnp.dot(y_ref[...], w_ref[...], preferred_element_type=jnp.float32)

    out = (branch(wg0_ref, bg0_ref, ya_ref, wa_ref) + branch(wg1_ref, bg1_ref, yb_ref, wb_ref)
           + branch(wg2_ref, bg2_ref, yc_ref, wc_ref))
    o_ref[...] = out.astype(o_ref.dtype)


def _merge(z, ya, yb, yc, w_gate_up, b_gate, wa, wb, wc, tm=1024, tn=512):
    t = z.shape[0]
    d = wa.shape[1]
    tm = min(tm, t)
    nj = d // tn
    assert OFF_GL % GATE_RANK == 0
    gl_blk = OFF_GL // GATE_RANK

    def wg_spec(i):
        return pl.BlockSpec((GATE_RANK, tn), lambda j, m, i=i: (0, i * nj + j))

    def bg_spec(i):
        return pl.BlockSpec((1, tn), lambda j, m, i=i: (0, i * nj + j))

    def row_spec(width):
        return pl.BlockSpec((tm, width), lambda j, m: (m, 0))

    def w_spec(width):
        return pl.BlockSpec((width, tn), lambda j, m: (0, j))

    bg = b_gate.reshape(1, N_BRANCH * d)
    return pl.pallas_call(
        _merge_kernel,
        grid=(nj, t // tm),
        in_specs=[pl.BlockSpec((tm, GATE_RANK), lambda j, m: (m, gl_blk)),
                  row_spec(MLSTM_WIDTH), row_spec(NSA_WIDTH), row_spec(RET_WIDTH),
                  wg_spec(0), wg_spec(1), wg_spec(2), bg_spec(0), bg_spec(1), bg_spec(2),
                  w_spec(MLSTM_WIDTH), w_spec(NSA_WIDTH), w_spec(RET_WIDTH)],
        out_specs=pl.BlockSpec((tm, tn), lambda j, m: (m, j)),
        out_shape=jax.ShapeDtypeStruct((t, d), jnp.bfloat16),
        compiler_params=_cparams(("parallel", "parallel")),
        name="gated_merge",
    )(z, ya, yb, yc, w_gate_up, w_gate_up, w_gate_up, bg, bg, bg, wa, wb, wc)


def _head_norm(h, gain):
    mu = jnp.mean(h, axis=-1, keepdims=True)
    d = h - mu
    var = jnp.mean(d * d, axis=-1, keepdims=True)
    return d * lax.rsqrt(var + EPS) * gain


def _lane_pick(x, lane_ids, lane):
    return jnp.sum(jnp.where(lane_ids == lane, x, 0.0), axis=1, keepdims=True)


def _mlstm_kernel(q_ref, k_ref, v_ref, og_ref, sm_ref, cq_ref, ck_ref, gain_ref, o_ref, *, n_chunks):
    L = CHUNK
    h = pl.program_id(1)
    lane_ids = lax.broadcasted_iota(jnp.int32, (L, SMALL_COLS), 1)
    row = lax.broadcasted_iota(jnp.int32, (L, L), 0)
    col = lax.broadcasted_iota(jnp.int32, (L, L), 1)
    causal = col <= row
    eye = col == row
    cq = cq_ref[...]
    ck = ck_ref[...]
    gain = gain_ref[...]
    scale = HEAD_DIM ** -0.5

    def conv_silu(ref, w, c, start):
        cur = ref[pl.ds(start, L), :]
        pstart = pl.multiple_of(jnp.maximum(start - 8, 0), 8)
        prev = jnp.where(c > 0, ref[pl.ds(pstart, 8), :], 0.0)
        ext = jnp.concatenate([prev, cur], axis=0)
        y = w[CONV_WIDTH - 1:CONV_WIDTH, :] * cur
        for kk in range(CONV_WIDTH - 1):
            off = 8 - (CONV_WIDTH - 1) + kk
            y = y + w[kk:kk + 1, :] * ext[off:off + L, :]
        return y * _sigmoid(y)

    def body(c, carry):
        cmat, nvec, m = carry
        start = pl.multiple_of(c * L, L)
        q = conv_silu(q_ref, cq, c, start)
        k = conv_silu(k_ref, ck, c, start) * scale
        v = v_ref[pl.ds(start, L), :]
        sm = sm_ref[pl.ds(start, L), :]
        i_col = _lane_pick(sm, lane_ids, SM_I + h)
        f_col = _lane_pick(sm, lane_ids, SM_F + h)
        ls_col = jnp.minimum(f_col, 0.0) - jnp.log(1.0 + jnp.exp(-jnp.abs(f_col)))
        a_row = jnp.sum(jnp.where(row <= col, ls_col, 0.0), axis=0, keepdims=True)
        a_col = jnp.sum(jnp.where(eye, a_row, 0.0), axis=1, keepdims=True)
        i_row = jnp.sum(jnp.where(eye, i_col, 0.0), axis=0, keepdims=True)
        g = jnp.sum(ls_col, axis=0, keepdims=True)
        w_col = g - a_col + i_col
        m_loc = jnp.max(w_col, axis=0, keepdims=True)
        e_col = jnp.exp(w_col - m_loc)

        log_d = jnp.where(causal, a_col - a_row + i_row, NEG_INF)
        log_inter = a_col + m
        m_row = jnp.maximum(log_inter, jnp.max(log_d, axis=1, keepdims=True))
        inter = jnp.exp(log_inter - m_row)
        dmat = jnp.exp(log_d - m_row)

        qb = q.astype(jnp.bfloat16)
        kb = k.astype(jnp.bfloat16)
        vb = v.astype(jnp.bfloat16)
        sc = lax.dot_general(qb, kb, _NT, preferred_element_type=jnp.float32) * dmat
        num = (inter * jnp.dot(qb, cmat.astype(jnp.bfloat16), preferred_element_type=jnp.float32)
               + jnp.dot(sc.astype(jnp.bfloat16), vb, preferred_element_type=jnp.float32))
        den = inter * jnp.sum(q * nvec, axis=1, keepdims=True) + jnp.sum(sc, axis=1, keepdims=True)
        hout = num / jnp.maximum(jnp.abs(den), jnp.exp(-m_row))

        y = _head_norm(hout, gain)
        o_ref[pl.ds(start, L), :] = (_sigmoid(og_ref[pl.ds(start, L), :]) * y).astype(o_ref.dtype)

        m_new = jnp.maximum(g + m, m_loc)
        a_old = jnp.exp(g + m - m_new)
        a_new = jnp.exp(m_loc - m_new)
        ke = k * e_col
        c_new = a_old * cmat + a_new * jnp.dot(ke.T.astype(jnp.bfloat16), vb, preferred_element_type=jnp.float32)
        n_new = a_old * nvec + a_new * jnp.sum(ke, axis=0, keepdims=True)
        return c_new, n_new, m_new

    init = (jnp.zeros((HEAD_DIM, HEAD_DIM), jnp.float32), jnp.zeros((1, HEAD_DIM), jnp.float32),
            jnp.zeros((1, 1), jnp.float32))
    lax.fori_loop(0, n_chunks, body, init)


def _mlstm(z, zs, conv_qk, gain, batch, seq):
    nh = MLSTM_HEADS
    hb = HEAD_DIM

    def seg(off):
        return pl.BlockSpec((seq, hb), lambda b, h, off=off: (b, off // hb + h))

    return pl.pallas_call(
        functools.partial(_mlstm_kernel, n_chunks=seq // CHUNK),
        grid=(batch, nh),
        in_specs=[seg(OFF_MQ), seg(OFF_MK), seg(OFF_MV), seg(OFF_MO),
                  pl.BlockSpec((seq, SMALL_COLS), lambda b, h: (b, 0)),
                  pl.BlockSpec((CONV_WIDTH, hb), lambda b, h: (0, h)),
                  pl.BlockSpec((CONV_WIDTH, hb), lambda b, h: (0, nh + h)),
                  pl.BlockSpec((1, hb), lambda b, h: (0, h))],
        out_specs=pl.BlockSpec((seq, hb), lambda b, h: (b, h)),
        out_shape=jax.ShapeDtypeStruct((batch * seq, MLSTM_WIDTH), jnp.bfloat16),
        compiler_params=_cparams(("parallel", "parallel")),
        name="mlstm",
    )(z, z, z, z, zs, conv_qk, conv_qk, gain.reshape(1, MLSTM_WIDTH))


def _retention_kernel(q_ref, k_ref, v_ref, g_ref, cos_ref, sin_ref, lg_ref, gain_ref, o_ref, *, n_chunks):
    L = CHUNK
    row = lax.broadcasted_iota(jnp.int32, (L, L), 0)
    col = lax.broadcasted_iota(jnp.int32, (L, L), 1)
    rel = (row - col).astype(jnp.float32)
    lg = lg_ref[0][:, 0:1]
    decay = jnp.where(rel >= 0, jnp.exp(lg * jnp.maximum(rel, 0.0)), 0.0)
    idx = lax.broadcasted_iota(jnp.int32, (L, 1), 0).astype(jnp.float32)
    xi = jnp.exp(lg * (idx + 1.0))
    zeta = jnp.exp(lg * (L - 1.0 - idx))
    chunk_decay = jnp.exp(lg * L)
    gain = gain_ref[...]
    scale = HEAD_DIM ** -0.5

    def body(c, rmat):
        start = pl.multiple_of(c * L, L)
        cosf = cos_ref[pl.ds(start, L), :]
        sinf = sin_ref[pl.ds(start, L), :]

        def rot(ref):
            x = ref[pl.ds(start, L), :]
            return x * cosf + pltpu.roll(x, HEAD_DIM // 2, 1) * sinf

        q = rot(q_ref)
        k = rot(k_ref) * scale
        vb = v_ref[pl.ds(start, L), :].astype(jnp.bfloat16)
        qb = q.astype(jnp.bfloat16)
        inner = lax.dot_general(qb, k.astype(jnp.bfloat16), _NT, preferred_element_type=jnp.float32) * decay
        out = (jnp.dot(inner.astype(jnp.bfloat16), vb, preferred_element_type=jnp.float32)
               + xi * jnp.dot(qb, rmat.astype(jnp.bfloat16), preferred_element_type=jnp.float32))
        y = _head_norm(out, gain)
        gate = g_ref[pl.ds(start, L), :]
        o_ref[pl.ds(start, L), :] = (gate * _sigmoid(gate) * y).astype(o_ref.dtype)
        kz = (k * zeta).T.astype(jnp.bfloat16)
        return chunk_decay * rmat + jnp.dot(kz, vb, preferred_element_type=jnp.float32)

    lax.fori_loop(0, n_chunks, body, jnp.zeros((HEAD_DIM, HEAD_DIM), jnp.float32))


def _retention(z, cosf, sinf, lg, gain, batch, seq):
    nh = RET_HEADS
    hb = HEAD_DIM

    def seg(off):
        return pl.BlockSpec((seq, hb), lambda b, h, off=off: (b, off // hb + h))

    table = pl.BlockSpec((seq, hb), lambda b, h: (0, 0))
    return pl.pallas_call(
        functools.partial(_retention_kernel, n_chunks=seq // CHUNK),
        grid=(batch, nh),
        in_specs=[seg(OFF_RQ), seg(OFF_RK), seg(OFF_RV), seg(OFF_RG), table, table,
                  pl.BlockSpec((1, 1, hb), lambda b, h: (h, 0, 0)),
                  pl.BlockSpec((1, hb), lambda b, h: (0, h))],
        out_specs=pl.BlockSpec((seq, hb), lambda b, h: (b, h)),
        out_shape=jax.ShapeDtypeStruct((batch * seq, RET_WIDTH), jnp.bfloat16),
        compiler_params=_cparams(("parallel", "parallel")),
        name="retention",
    )(z, z, z, z, cosf, sinf, lg, gain.reshape(1, RET_WIDTH))


def _gelu_tanh(x):
    return 0.5 * x * (1.0 + jnp.tanh(math.sqrt(2.0 / math.pi) * (x + 0.044715 * (x * x * x))))


def _compress_kernel(k_ref, v_ref, pk_ref, w1k_ref, w2k_ref, pv_ref, w1v_ref, w2v_ref, ko_ref, vo_ref, pad_ref,
                     *, seq):
    nblk = seq // CMP_STRIDE
    for x_ref, p_ref, w1_ref, w2_ref, o_ref, transposed in ((k_ref, pk_ref, w1k_ref, w2k_ref, ko_ref, False),
                                                             (v_ref, pv_ref, w1v_ref, w2v_ref, vo_ref, True)):
        pad_ref[0:seq, :] = x_ref[...]
        pad_ref[seq:seq + CMP_BLOCK, :] = jnp.zeros((CMP_BLOCK, HEAD_DIM), jnp.float32)
        acc = jnp.zeros((nblk, HEAD_DIM), jnp.float32)
        for p in range(CMP_BLOCK):
            xs = pad_ref[pl.ds(p, nblk, stride=CMP_STRIDE), :] + p_ref[p:p + 1, :]
            acc = acc + jnp.dot(xs.astype(jnp.bfloat16), w1_ref[p * HEAD_DIM:(p + 1) * HEAD_DIM, :],
                                preferred_element_type=jnp.float32)
        mid = _gelu_tanh(acc).astype(jnp.bfloat16)
        if transposed:
            out = lax.dot_general(w2_ref[...], mid, _NT, preferred_element_type=jnp.float32)
        else:
            out = jnp.dot(mid, w2_ref[...], preferred_element_type=jnp.float32)
        o_ref[0, 0] = out.astype(o_ref.dtype)


def _compress(z, pos_k, w1_k, w2_k, pos_v, w1_v, w2_v, batch, seq):
    hb = HEAD_DIM
    nblk = seq // CMP_STRIDE
    cmp_in = CMP_BLOCK * hb

    def seg(off):
        return pl.BlockSpec((seq, hb), lambda b, g, off=off: (b, off // hb + g))

    def full(shape):
        return pl.BlockSpec(shape, lambda b, g: (0,) * len(shape))

    out_specs = [pl.BlockSpec((1, 1, nblk, hb), lambda b, g: (b, g, 0, 0)),
                 pl.BlockSpec((1, 1, hb, nblk), lambda b, g: (b, g, 0, 0))]
    out_shape = [jax.ShapeDtypeStruct((batch, NSA_KV_GROUPS, nblk, hb), jnp.bfloat16),
                 jax.ShapeDtypeStruct((batch, NSA_KV_GROUPS, hb, nblk), jnp.bfloat16)]
    return pl.pallas_call(
        functools.partial(_compress_kernel, seq=seq),
        grid=(batch, NSA_KV_GROUPS),
        in_specs=[seg(OFF_NKC), seg(OFF_NVC),
                  full((CMP_BLOCK, hb)), full((cmp_in, hb)), full((hb, hb)),
                  full((CMP_BLOCK, hb)), full((cmp_in, hb)), full((hb, hb))],
        out_specs=out_specs,
        out_shape=out_shape,
        scratch_shapes=[pltpu.VMEM((seq + CMP_BLOCK, hb), jnp.float32)],
        compiler_params=_cparams(("parallel", "parallel")),
        name="nsa_compress",
    )(z, z, pos_k, w1_k, w2_k, pos_v, w1_v, w2_v)


def _rel_bucket_np(dist):
    n = np.maximum(dist, 0)
    exact = REL_BUCKETS // 2
    log_ratio = (np.log(np.maximum(n, 1).astype(np.float32) / np.float32(exact))
                 / np.float32(math.log(REL_MAX_DIST / exact)))
    large = np.minimum(exact + (log_ratio * np.float32(REL_BUCKETS - exact)).astype(np.int32), REL_BUCKETS - 1)
    return np.where(n < exact, n, large).astype(np.int32)


def _bias_kernel(tbl_ref, idxc_ref, idxt_ref, bc_ref, bt_ref):
    h = pl.program_id(0)
    idxc = idxc_ref[...]
    idxt = idxt_ref[...]
    accc = jnp.zeros(idxc.shape, jnp.float32)
    acct = jnp.zeros(idxt.shape, jnp.float32)
    for b in range(REL_BUCKETS):
        val = tbl_ref[b, h]
        accc = jnp.where(idxc == b, val, accc)
        acct = jnp.where(idxt == b, val, acct)
    bc_ref[0] = accc
    bt_ref[:, 0] = acct


def _bias_tables(rel_bias, seq):
    nblk = seq // CMP_STRIDE
    j = np.arange(nblk)[:, None]
    t = np.arange(seq)[None, :]
    idx_c = _rel_bucket_np(t - (j * CMP_STRIDE + CMP_BLOCK - 1))
    kk = np.arange(TILE)[:, None]
    i = np.arange(TILE)[None, :]
    idx_t = np.stack([_rel_bucket_np(i - kk + d * TILE) for d in range(3)])
    return pl.pallas_call(
        _bias_kernel,
        grid=(NSA_HEADS,),
        in_specs=[pl.BlockSpec(memory_space=pltpu.SMEM),
                  pl.BlockSpec((nblk, seq), lambda h: (0, 0)),
                  pl.BlockSpec((3, TILE, TILE), lambda h: (0, 0, 0))],
        out_specs=[pl.BlockSpec((1, nblk, seq), lambda h: (h, 0, 0)),
                   pl.BlockSpec((3, 1, TILE, TILE), lambda h: (0, h, 0, 0))],
        out_shape=[jax.ShapeDtypeStruct((NSA_HEADS, nblk, seq), jnp.float32),
                   jax.ShapeDtypeStruct((3, NSA_HEADS, TILE, TILE), jnp.float32)],
        compiler_params=_cparams(("parallel",)),
        name="nsa_bias_tables",
    )(rel_bias, jnp.asarray(idx_c), jnp.asarray(idx_t))


def _nsa_kernel(q_ref, kc_ref, vct_ref, ks_ref, vs_ref, kw_ref, vw_ref, sm_ref, bc_ref, bt_ref, o_ref,
                kb_ref, vt_ref, acc_ref, m_ref, l_ref, sel_ref, gate_ref, *, seq):
    R = NSA_GROUP_SIZE
    T = TILE
    nblk = seq // CMP_STRIDE
    n_sel = seq // SEL_BLOCK
    n_tiles = seq // T
    top = min(SEL_TOPK, n_sel)
    bf = jnp.bfloat16
    g = pl.program_id(1)
    n = pl.program_id(2)
    t0 = n * T

    @pl.when(n == 0)
    def _():
        for slot, (k_ref, v_ref) in enumerate(((ks_ref, vs_ref), (kw_ref, vw_ref))):
            for c in range(n_tiles):
                kb_ref[slot, c] = k_ref[c * T:(c + 1) * T, :].astype(bf)
                vt_ref[slot, c] = v_ref[c * T:(c + 1) * T, :].T.astype(bf)

    q = q_ref[...] * (HEAD_DIM ** -0.5)
    qt = jnp.concatenate([q[:, r * HEAD_DIM:(r + 1) * HEAD_DIM].T for r in range(R)], axis=1).astype(bf)

    def rep(x):
        return jnp.concatenate([x] * R, axis=1)

    j_c = lax.broadcasted_iota(jnp.int32, (nblk, T), 0)
    t_c = t0 + lax.broadcasted_iota(jnp.int32, (nblk, T), 1)
    valid_c = rep(t_c >= j_c * CMP_STRIDE + (CMP_BLOCK - 1))
    s_c = (jnp.dot(kc_ref[0, 0], qt, preferred_element_type=jnp.float32)
           + jnp.concatenate([bc_ref[r] for r in range(R)], axis=1))
    s_c = jnp.where(valid_c, s_c, NEG_INF)
    e_c = jnp.exp(s_c - jnp.max(s_c, axis=0, keepdims=True))
    p_c = jnp.where(valid_c, e_c * (1.0 / jnp.sum(e_c, axis=0, keepdims=True)), 0.0)
    o_c = jnp.dot(vct_ref[0, 0], p_c.astype(bf), preferred_element_type=jnp.float32)

    p_sum = p_c[:, 0:T]
    for r in range(1, R):
        p_sum = p_sum + p_c[:, r * T:(r + 1) * T]
    ov_s = lax.broadcasted_iota(jnp.int32, (T, nblk), 0)
    ov_j = lax.broadcasted_iota(jnp.int32, (T, nblk), 1)
    ratio = SEL_BLOCK // CMP_STRIDE
    span = CMP_BLOCK // CMP_STRIDE
    overlap = ((ov_j < ratio * ov_s + ratio) & (ov_j + span > ratio * ov_s) & (ov_s < n_sel)).astype(jnp.float32)
    imp = jnp.dot(overlap, p_sum, preferred_element_type=jnp.float32, precision=lax.Precision.HIGHEST)

    rows = lax.broadcasted_iota(jnp.int32, (T, T), 0)
    t_q = t0 + lax.broadcasted_iota(jnp.int32, (T, T), 1)
    cur = t_q // SEL_BLOCK
    forced = (rows == 0) | (rows == cur) | (rows == cur - 1)
    val = jnp.where(forced, jnp.inf, jnp.where((rows > cur) | (rows >= n_sel), -jnp.inf, imp))
    blk_f = rows.astype(jnp.float32)
    sel = jnp.zeros((T, T), jnp.float32)
    for _ in range(top):
        best = jnp.max(val, axis=0, keepdims=True)
        first = jnp.min(jnp.where(val == best, blk_f, float(T)), axis=0, keepdims=True)
        hit = blk_f == first
        sel = jnp.where(hit, 1.0, sel)
        val = jnp.where(hit, -jnp.inf, val)
    sel_ref[...] = sel

    m_ref[...] = jnp.full(m_ref.shape, M_FLOOR, jnp.float32)
    l_ref[...] = jnp.zeros(l_ref.shape, jnp.float32)
    acc_ref[...] = jnp.zeros(acc_ref.shape, jnp.float32)

    def scores(kv, c, allowed):
        mask_add = jnp.where(allowed, 0.0, NEG_INF)
        c_read = jnp.minimum(c, n)
        tile_dist = jnp.minimum(n - c_read, 2)
        bias = jnp.concatenate([bt_ref[tile_dist, r] + mask_add for r in range(R)], axis=1)
        return jnp.dot(kb_ref[kv, c_read], qt, preferred_element_type=jnp.float32) + bias, c_read

    def fold(slot, kv, s, c_read):
        m_old = m_ref[slot]
        m_new = jnp.maximum(m_old, jnp.max(s, axis=0, keepdims=True))
        p = jnp.exp(s - m_new)
        alpha = jnp.exp(m_old - m_new)
        l_ref[slot] = alpha * l_ref[slot] + jnp.sum(p, axis=0, keepdims=True)
        acc_ref[slot] = alpha * acc_ref[slot] + jnp.dot(vt_ref[kv, c_read], p.astype(bf),
                                                        preferred_element_type=jnp.float32)
        m_ref[slot] = m_new

    blocks_per_tile = T // SEL_BLOCK

    def sel_scores(c):
        kpos = c * T + rows
        chosen = sel_ref[pl.ds(c * blocks_per_tile, 1), :]
        for i in range(1, blocks_per_tile):
            chosen = jnp.where(rows < i * SEL_BLOCK, chosen, sel_ref[pl.ds(c * blocks_per_tile + i, 1), :])
        return scores(0, c, (chosen > 0.5) & (kpos <= t_q))

    def win_scores(c):
        dist = t_q - (c * T + rows)
        return scores(1, c, (dist >= 0) & (dist < WINDOW))

    win_lo = jnp.maximum(n - WINDOW // T, 0)
    trips = jnp.maximum((n + 2) // 2, n - win_lo + 1)

    def body(i, carry):
        s_even = sel_scores(2 * i)
        s_odd = sel_scores(2 * i + 1)
        s_win = win_scores(win_lo + i)
        fold(0, 0, *s_even)
        fold(1, 0, *s_odd)
        fold(2, 1, *s_win)
        return carry

    lax.fori_loop(0, trips, body, 0)

    m_s = jnp.maximum(m_ref[0], m_ref[1])
    w_even = jnp.exp(m_ref[0] - m_s)
    w_odd = jnp.exp(m_ref[1] - m_s)
    o_s = (acc_ref[0] * w_even + acc_ref[1] * w_odd) * (1.0 / (l_ref[0] * w_even + l_ref[1] * w_odd))
    o_w = acc_ref[2] * (1.0 / l_ref[2])

    gate_ref[...] = _sigmoid(sm_ref[...]).T
    outs = []
    for r in range(R):
        def gate(branch):
            return gate_ref[pl.ds(SM_GATE + branch * NSA_HEADS + g * R + r, 1), :]
        sl = slice(r * T, (r + 1) * T)
        outs.append((gate(0) * o_c[:, sl] + gate(1) * o_s[:, sl] + gate(2) * o_w[:, sl]).T)
    o_ref[...] = jnp.concatenate(outs, axis=1).astype(o_ref.dtype)


def _nsa(z, zs, k_c, v_c, bias_c, bias_t, batch, seq):
    hb = HEAD_DIM
    R = NSA_GROUP_SIZE
    nblk = seq // CMP_STRIDE
    nq = seq // TILE
    gw = R * hb
    assert OFF_NQ % gw == 0

    def seg(off):
        return pl.BlockSpec((seq, hb), lambda b, g, n, off=off: (b, off // hb + g))

    return pl.pallas_call(
        functools.partial(_nsa_kernel, seq=seq),
        grid=(batch, NSA_KV_GROUPS, nq),
        in_specs=[pl.BlockSpec((TILE, gw), lambda b, g, n: (b * nq + n, OFF_NQ // gw + g)),
                  pl.BlockSpec((1, 1, nblk, hb), lambda b, g, n: (b, g, 0, 0)),
                  pl.BlockSpec((1, 1, hb, nblk), lambda b, g, n: (b, g, 0, 0)),
                  seg(OFF_NKS), seg(OFF_NVS), seg(OFF_NKW), seg(OFF_NVW),
                  pl.BlockSpec((TILE, SMALL_COLS), lambda b, g, n: (b * nq + n, 0)),
                  pl.BlockSpec((R, nblk, TILE), lambda b, g, n: (g, 0, n)),
                  pl.BlockSpec((3, R, TILE, TILE), lambda b, g, n: (0, g, 0, 0))],
        out_specs=pl.BlockSpec((TILE, gw), lambda b, g, n: (b * nq + n, g)),
        out_shape=jax.ShapeDtypeStruct((batch * seq, NSA_WIDTH), jnp.bfloat16),
        scratch_shapes=[pltpu.VMEM((2, nq, TILE, hb), jnp.bfloat16),
                        pltpu.VMEM((2, nq, hb, TILE), jnp.bfloat16),
                        pltpu.VMEM((3, hb, R * TILE), jnp.float32),
                        pltpu.VMEM((3, 1, R * TILE), jnp.float32),
                        pltpu.VMEM((3, 1, R * TILE), jnp.float32),
                        pltpu.VMEM((TILE, TILE), jnp.float32),
                        pltpu.VMEM((SMALL_COLS, TILE), jnp.float32)],
        compiler_params=_cparams(("parallel", "parallel", "arbitrary")),
        name="nsa_attention",
    )(z, k_c, v_c, z, z, z, z, zs, bias_c, bias_t)


def _rope_tables(seq):
    half = HEAD_DIM // 2
    inv_freq = 1.0 / (10000.0 ** jnp.linspace(0.0, 1.0, half))
    ang = jnp.arange(seq).astype(jnp.float32)[:, None] * inv_freq[None, :]
    cos, sin = jnp.cos(ang), jnp.sin(ang)
    return jnp.concatenate([cos, cos], axis=-1), jnp.concatenate([-sin, sin], axis=-1)


def _retention_log_decay():
    lg = np.log1p(-np.exp2(-5.0 - np.arange(RET_HEADS, dtype=np.float64))).astype(np.float32)
    return jnp.asarray(np.broadcast_to(lg[:, None, None], (RET_HEADS, 1, HEAD_DIM)).copy())


def _split_in_proj(w_in, b_in, w_gate_down):
    bf = jnp.bfloat16
    a0 = 4 * MLSTM_WIDTH
    a1 = a0 + 2 * MLSTM_HEADS
    a2 = a1 + NSA_WIDTH + 6 * NSA_KV_WIDTH
    a3 = a2 + 3 * NSA_HEADS
    d = w_in.shape[0]
    w_main = jnp.concatenate([w_gate_down.astype(bf), w_in[:, :a0].astype(bf), w_in[:, a1:a2].astype(bf),
                              w_in[:, a3:].astype(bf)], axis=1)
    b_main = jnp.concatenate([jnp.zeros((GATE_RANK,), jnp.float32), b_in[:a0], b_in[a1:a2], b_in[a3:]])
    n_small = (a1 - a0) + (a3 - a2)
    w_small = jnp.concatenate([w_in[:, a0:a1].astype(bf), w_in[:, a2:a3].astype(bf),
                               jnp.zeros((d, SMALL_COLS - n_small), bf)], axis=1)
    b_small = jnp.concatenate([b_in[a0:a1], b_in[a2:a3], jnp.zeros((SMALL_COLS - n_small,), jnp.float32)])
    return w_main, b_main, w_small, b_small


def _token_mixer(h, x_res, batch, seq, bias_c, bias_t, cosf, sinf, lg, w_in, b_in, conv_qk, mlstm_norm_g,
                 cmp_pos_k, cmp_w1_k, cmp_w2_k, cmp_pos_v, cmp_w1_v, cmp_w2_v, ret_norm_g,
                 w_br_mlstm, w_br_nsa, w_br_ret, w_gate_down, w_gate_up, b_gate, w_out):
    bf = jnp.bfloat16
    w_main, b_main, w_small, b_small = _split_in_proj(w_in, b_in, w_gate_down)
    z = _matmul(h, w_main, bias=b_main, out_dtype=jnp.float32, tm=1024, tn=512, tk=D_MODEL, name="in_proj")
    zs = _matmul(h, w_small, bias=b_small, out_dtype=jnp.float32, tm=1024, tn=SMALL_COLS, tk=D_MODEL,
                 name="in_proj_small")
    y_a = _mlstm(z, zs, conv_qk, mlstm_norm_g, batch, seq)
    k_c, v_c = _compress(z, cmp_pos_k, cmp_w1_k.astype(bf), cmp_w2_k.astype(bf),
                         cmp_pos_v, cmp_w1_v.astype(bf), cmp_w2_v.T.astype(bf), batch, seq)
    y_b = _nsa(z, zs, k_c, v_c, bias_c, bias_t, batch, seq)
    y_c = _retention(z, cosf, sinf, lg, ret_norm_g, batch, seq)
    merged = _merge(z, y_a, y_b, y_c, w_gate_up.astype(bf), b_gate,
                    w_br_mlstm.astype(bf), w_br_nsa.astype(bf), w_br_ret.astype(bf))
    return _matmul(merged, w_out.astype(bf), res=x_res, out_dtype=jnp.float32, tm=1024, tn=512, tk=D_MODEL,
                   name="out_proj")


def kernel(x, rel_bias, norm_mix_g, w_in, b_in, conv_qk, mlstm_norm_g, cmp_pos_k, cmp_w1_k, cmp_w2_k, cmp_pos_v, cmp_w1_v, cmp_w2_v, ret_norm_g, w_br_mlstm, w_br_nsa, w_br_ret, w_gate_down, w_gate_up, b_gate, w_out, norm_mlp_g, w_up, w_down, final_norm_g):
    batch, seq, d = x.shape
    depth = w_in.shape[0]
    bf = jnp.bfloat16
    bias_c, bias_t = _bias_tables(rel_bias, seq)
    cosf, sinf = _rope_tables(seq)
    lg = _retention_log_decay()
    xr = x.reshape(batch * seq, d)
    for l in range(depth):
        h = _rmsnorm(xr, norm_mix_g[l], bf)
        xr = _token_mixer(h, xr, batch, seq, bias_c, bias_t, cosf, sinf, lg, w_in[l], b_in[l], conv_qk[l],
                          mlstm_norm_g[l], cmp_pos_k[l], cmp_w1_k[l], cmp_w2_k[l], cmp_pos_v[l], cmp_w1_v[l],
                          cmp_w2_v[l], ret_norm_g[l], w_br_mlstm[l], w_br_nsa[l], w_br_ret[l],
                          w_gate_down[l], w_gate_up[l], b_gate[l], w_out[l])
        h = _rmsnorm(xr, norm_mlp_g[l], bf)
        up = _matmul(h, w_up[l].astype(bf), act="relu2", out_dtype=bf, tm=1024, tn=1024, tk=D_MODEL, name="mlp_up")
        xr = _matmul(up, w_down[l].astype(bf), res=xr, out_dtype=jnp.float32, tm=1024, tn=512, tk=4096,
                     name="mlp_down")
    return _rmsnorm(xr, final_norm_g, jnp.float32).reshape(batch, seq, d)
```

```python
import functools
import math

import numpy as np
import jax
import jax.numpy as jnp
from jax import lax
from jax.experimental import pallas as pl
from jax.experimental.pallas import tpu as pltpu

D_MODEL = 4096
HEAD_DIM = 128
MIX_WIDTH = D_MODEL // 2
MLSTM_HEADS = MIX_WIDTH // (4 * HEAD_DIM)
NSA_HEADS = MIX_WIDTH // (2 * HEAD_DIM)
RET_HEADS = MIX_WIDTH // (4 * HEAD_DIM)
NSA_KV_GROUPS = 2
NSA_GROUP_SIZE = NSA_HEADS // NSA_KV_GROUPS
MLSTM_WIDTH = MLSTM_HEADS * HEAD_DIM
NSA_WIDTH = NSA_HEADS * HEAD_DIM
RET_WIDTH = RET_HEADS * HEAD_DIM
NSA_KV_WIDTH = NSA_KV_GROUPS * HEAD_DIM
CONV_WIDTH = 4
CMP_BLOCK = 32
CMP_STRIDE = 16
SEL_BLOCK = 64
SEL_TOPK = 8
WINDOW = 512
REL_BUCKETS = 32
REL_MAX_DIST = 128
GATE_RANK = D_MODEL // 4
N_BRANCH = 3
EPS = 1e-6
NEG_INF = -1e30

CHUNK = 128
TILE = 128
SEL_GROUP = 4

OFF_GL = 0
OFF_MQ = OFF_GL + GATE_RANK
OFF_MK = OFF_MQ + MLSTM_WIDTH
OFF_MV = OFF_MK + MLSTM_WIDTH
OFF_MO = OFF_MV + MLSTM_WIDTH
OFF_NQ = OFF_MO + MLSTM_WIDTH
OFF_NKC = OFF_NQ + NSA_WIDTH
OFF_NVC = OFF_NKC + NSA_KV_WIDTH
OFF_NKS = OFF_NVC + NSA_KV_WIDTH
OFF_NVS = OFF_NKS + NSA_KV_WIDTH
OFF_NKW = OFF_NVS + NSA_KV_WIDTH
OFF_NVW = OFF_NKW + NSA_KV_WIDTH
OFF_RQ = OFF_NVW + NSA_KV_WIDTH
OFF_RK = OFF_RQ + RET_WIDTH
OFF_RV = OFF_RK + RET_WIDTH
OFF_RG = OFF_RV + RET_WIDTH
MAIN_COLS = OFF_RG + RET_WIDTH
SMALL_COLS = 128
SM_I = 0
SM_F = MLSTM_HEADS
SM_GATE = 2 * MLSTM_HEADS

VMEM_LIMIT = 56 * 1024 * 1024

_NT = (((1,), (1,)), ((), ()))


def _sigmoid(x):
    return 1.0 / (1.0 + jnp.exp(-x))


def _cparams(sem):
    return pltpu.CompilerParams(dimension_semantics=sem, vmem_limit_bytes=VMEM_LIMIT)


def _rmsnorm_kernel(x_ref, g_ref, o_ref):
    x = x_ref[...]
    ms = jnp.mean(x * x, axis=-1, keepdims=True)
    o_ref[...] = (x * lax.rsqrt(ms + EPS) * g_ref[...]).astype(o_ref.dtype)


def _rmsnorm(x, gain, out_dtype, tm=256):
    t, d = x.shape
    return pl.pallas_call(
        _rmsnorm_kernel,
        grid=(t // tm,),
        in_specs=[pl.BlockSpec((tm, d), lambda i: (i, 0)),
                  pl.BlockSpec((1, d), lambda i: (0, 0))],
        out_specs=pl.BlockSpec((tm, d), lambda i: (i, 0)),
        out_shape=jax.ShapeDtypeStruct((t, d), out_dtype),
        compiler_params=_cparams(("parallel",)),
        name="rmsnorm",
    )(x, gain.reshape(1, d))


def _mm_kernel(*refs, nk, act, has_bias, has_res, cast_w):
    a_ref, w_ref = refs[0], refs[1]
    pos = 2
    b_ref = refs[pos] if has_bias else None
    pos += int(has_bias)
    r_ref = refs[pos] if has_res else None
    pos += int(has_res)
    o_ref = refs[pos]
    pos += 1
    acc_ref = refs[pos] if nk > 1 else None
    pos += int(nk > 1)

    if cast_w:
        wb_ref = refs[pos]

        @pl.when(pl.program_id(1) == 0)
        def _():
            wb_ref[...] = w_ref[...].astype(jnp.bfloat16)

        w_ref = wb_ref

    part = jnp.dot(a_ref[...], w_ref[...], preferred_element_type=jnp.float32)

    def finish(acc):
        if has_bias:
            acc = acc + b_ref[...]
        if act == "relu2":
            acc = jnp.square(jnp.maximum(acc, 0.0))
        if has_res:
            acc = acc + r_ref[...]
        o_ref[...] = acc.astype(o_ref.dtype)

    if nk == 1:
        finish(part)
    else:
        k = pl.program_id(2)

        @pl.when(k == 0)
        def _():
            acc_ref[...] = part

        @pl.when(k > 0)
        def _():
            acc_ref[...] += part

        @pl.when(k == nk - 1)
        def _():
            finish(acc_ref[...])


def _matmul(a, w, layer, *, bias=None, res=None, act=None, out_dtype, tm, tn, tk, name):
    m, kdim = a.shape
    n = w.shape[2]
    tm, tn, tk = min(tm, m), min(tn, n), min(tk, kdim)
    nk = kdim // tk
    cast_w = w.dtype != jnp.bfloat16
    assert not (cast_w and nk > 1)
    in_specs = [pl.BlockSpec((tm, tk), lambda j, i, k: (i, k)),
                pl.BlockSpec((None, tk, tn), lambda j, i, k: (layer, k, j))]
    args = [a, w]
    if bias is not None:
        in_specs.append(pl.BlockSpec((None, 1, tn), lambda j, i, k: (layer, 0, j)))
        args.append(bias.reshape(bias.shape[0], 1, n))
    if res is not None:
        in_specs.append(pl.BlockSpec((tm, tn), lambda j, i, k: (i, j)))
        args.append(res)
    scratch = [pltpu.VMEM((tm, tn), jnp.float32)] if nk > 1 else []
    if cast_w:
        scratch.append(pltpu.VMEM((tk, tn), jnp.bfloat16))
    return pl.pallas_call(
        functools.partial(_mm_kernel, nk=nk, act=act, has_bias=bias is not None, has_res=res is not None,
                          cast_w=cast_w),
        grid=(n // tn, m // tm, nk),
        in_specs=in_specs,
        out_specs=pl.BlockSpec((tm, tn), lambda j, i, k: (i, j)),
        out_shape=jax.ShapeDtypeStruct((m, n), out_dtype),
        scratch_shapes=scratch,
        compiler_params=_cparams(("parallel", "arbitrary", "arbitrary")),
        name=name,
    )(*args)


def _merge_kernel(gl_ref, ya_ref, yb_ref, yc_ref, wg0_ref, wg1_ref, wg2_ref, bg0_ref, bg1_ref, bg2_ref,
                  wa_ref, wb_ref, wc_ref, o_ref, *wbf_refs):
    @pl.when(pl.program_id(1) == 0)
    def _():
        for src, dst in zip((wg0_ref, wg1_ref, wg2_ref, wa_ref, wb_ref, wc_ref), wbf_refs):
            dst[...] = src[...].astype(jnp.bfloat16)

    wg0, wg1, wg2, wa, wb, wc = wbf_refs
    gl = gl_ref[...].astype(jnp.bfloat16)

    def branch(wg, bg_ref, y_ref, w):
        gate = _sigmoid(jnp.dot(gl, wg[...], preferred_element_type=jnp.float32) + bg_ref[...])
        return gate * jnp.dot(y_ref[...], w[...], preferred_element_type=jnp.float32)

    out = branch(wg0, bg0_ref, ya_ref, wa) + branch(wg1, bg1_ref, yb_ref, wb) + branch(wg2, bg2_ref, yc_ref, wc)
    o_ref[...] = out.astype(o_ref.dtype)


def _merge(z, ya, yb, yc, layer, w_gate_up, b_gate, wa, wb, wc, tm=1024, tn=512):
    t = z.shape[0]
    d = wa.shape[2]
    tm = min(tm, t)
    nj = d // tn
    assert OFF_GL % GATE_RANK == 0
    gl_blk = OFF_GL // GATE_RANK

    def wg_spec(i):
        return pl.BlockSpec((None, GATE_RANK, tn), lambda j, m, i=i: (layer, 0, i * nj + j))

    def bg_spec(i):
        return pl.BlockSpec((None, 1, tn), lambda j, m, i=i: (layer, 0, i * nj + j))

    def row_spec(width):
        return pl.BlockSpec((tm, width), lambda j, m: (m, 0))

    def w_spec(width):
        return pl.BlockSpec((None, width, tn), lambda j, m: (layer, 0, j))

    bg = b_gate.reshape(b_gate.shape[0], 1, N_BRANCH * d)
    bf_scratch = [pltpu.VMEM((rows, tn), jnp.bfloat16)
                  for rows in (GATE_RANK, GATE_RANK, GATE_RANK, MLSTM_WIDTH, NSA_WIDTH, RET_WIDTH)]
    return pl.pallas_call(
        _merge_kernel,
        grid=(nj, t // tm),
        in_specs=[pl.BlockSpec((tm, GATE_RANK), lambda j, m: (m, gl_blk)),
                  row_spec(MLSTM_WIDTH), row_spec(NSA_WIDTH), row_spec(RET_WIDTH),
                  wg_spec(0), wg_spec(1), wg_spec(2), bg_spec(0), bg_spec(1), bg_spec(2),
                  w_spec(MLSTM_WIDTH), w_spec(NSA_WIDTH), w_spec(RET_WIDTH)],
        out_specs=pl.BlockSpec((tm, tn), lambda j, m: (m, j)),
        out_shape=jax.ShapeDtypeStruct((t, d), jnp.bfloat16),
        scratch_shapes=bf_scratch,
        compiler_params=_cparams(("parallel", "arbitrary")),
        name="gated_merge",
    )(z, ya, yb, yc, w_gate_up, w_gate_up, w_gate_up, bg, bg, bg, wa, wb, wc)


def _head_norm(h, gain):
    mu = jnp.mean(h, axis=-1, keepdims=True)
    d = h - mu
    var = jnp.mean(d * d, axis=-1, keepdims=True)
    return d * lax.rsqrt(var + EPS) * gain


def _mlstm_kernel(q_ref, k_ref, v_ref, og_ref, sm_ref, cq_ref, ck_ref, gain_ref, o_ref, *, n_chunks):
    L = CHUNK
    h = pl.program_id(1)
    lane_ids = lax.broadcasted_iota(jnp.int32, (L, SMALL_COLS), 1)
    row = lax.broadcasted_iota(jnp.int32, (L, L), 0)
    col = lax.broadcasted_iota(jnp.int32, (L, L), 1)
    causal = col <= row
    eye = col == row
    cq = cq_ref[...]
    ck = ck_ref[...]
    gain = gain_ref[...]
    scale = HEAD_DIM ** -0.5

    def conv_silu(ref, w, c, start):
        cur = ref[pl.ds(start, L), :]
        pstart = pl.multiple_of(jnp.maximum(start - 8, 0), 8)
        prev = jnp.where(c > 0, ref[pl.ds(pstart, 8), :], 0.0)
        ext = jnp.concatenate([prev, cur], axis=0)
        y = w[CONV_WIDTH - 1:CONV_WIDTH, :] * cur
        for kk in range(CONV_WIDTH - 1):
            off = 8 - (CONV_WIDTH - 1) + kk
            y = y + w[kk:kk + 1, :] * ext[off:off + L, :]
        return y * _sigmoid(y)

    def lane_pick(x, lane):
        return jnp.sum(jnp.where(lane_ids == lane, x, 0.0), axis=1, keepdims=True)

    def body(c, carry):
        cmat, nvec, m = carry
        start = pl.multiple_of(c * L, L)
        q = conv_silu(q_ref, cq, c, start)
        k = conv_silu(k_ref, ck, c, start) * scale
        v = v_ref[pl.ds(start, L), :]
        sm = sm_ref[pl.ds(start, L), :]
        i_col = lane_pick(sm, SM_I + h)
        f_col = lane_pick(sm, SM_F + h)
        ls_col = jnp.minimum(f_col, 0.0) - jnp.log(1.0 + jnp.exp(-jnp.abs(f_col)))
        a_row = jnp.sum(jnp.where(row <= col, ls_col, 0.0), axis=0, keepdims=True)
        a_col = jnp.sum(jnp.where(eye, a_row, 0.0), axis=1, keepdims=True)
        i_row = jnp.sum(jnp.where(eye, i_col, 0.0), axis=0, keepdims=True)
        g = jnp.sum(ls_col, axis=0, keepdims=True)
        w_col = g - a_col + i_col
        m_loc = jnp.max(w_col, axis=0, keepdims=True)
        e_col = jnp.exp(w_col - m_loc)

        log_d = jnp.where(causal, a_col - a_row + i_row, NEG_INF)
        log_inter = a_col + m
        m_row = jnp.maximum(log_inter, jnp.max(log_d, axis=1, keepdims=True))
        inter = jnp.exp(log_inter - m_row)
        dmat = jnp.exp(log_d - m_row)

        qb = q.astype(jnp.bfloat16)
        kb = k.astype(jnp.bfloat16)
        vb = v.astype(jnp.bfloat16)
        sc = lax.dot_general(qb, kb, _NT, preferred_element_type=jnp.float32) * dmat
        num = (inter * jnp.dot(qb, cmat.astype(jnp.bfloat16), preferred_element_type=jnp.float32)
               + jnp.dot(sc.astype(jnp.bfloat16), vb, preferred_element_type=jnp.float32))
        den = inter * jnp.sum(q * nvec, axis=1, keepdims=True) + jnp.sum(sc, axis=1, keepdims=True)
        hout = num / jnp.maximum(jnp.abs(den), jnp.exp(-m_row))

        y = _head_norm(hout, gain)
        o_ref[pl.ds(start, L), :] = (_sigmoid(og_ref[pl.ds(start, L), :]) * y).astype(o_ref.dtype)

        m_new = jnp.maximum(g + m, m_loc)
        a_old = jnp.exp(g + m - m_new)
        a_new = jnp.exp(m_loc - m_new)
        ke = k * e_col
        c_new = a_old * cmat + a_new * jnp.dot(ke.T.astype(jnp.bfloat16), vb, preferred_element_type=jnp.float32)
        n_new = a_old * nvec + a_new * jnp.sum(ke, axis=0, keepdims=True)
        return c_new, n_new, m_new

    init = (jnp.zeros((HEAD_DIM, HEAD_DIM), jnp.float32), jnp.zeros((1, HEAD_DIM), jnp.float32),
            jnp.zeros((1, 1), jnp.float32))
    lax.fori_loop(0, n_chunks, body, init)


def _mlstm(z, zs, conv_qk, gain, batch, seq):
    nh = MLSTM_HEADS
    hb = HEAD_DIM

    def seg(off):
        return pl.BlockSpec((seq, hb), lambda b, h, off=off: (b, off // hb + h))

    return pl.pallas_call(
        functools.partial(_mlstm_kernel, n_chunks=seq // CHUNK),
        grid=(batch, nh),
        in_specs=[seg(OFF_MQ), seg(OFF_MK), seg(OFF_MV), seg(OFF_MO),
                  pl.BlockSpec((seq, SMALL_COLS), lambda b, h: (b, 0)),
                  pl.BlockSpec((CONV_WIDTH, hb), lambda b, h: (0, h)),
                  pl.BlockSpec((CONV_WIDTH, hb), lambda b, h: (0, nh + h)),
                  pl.BlockSpec((1, hb), lambda b, h: (0, h))],
        out_specs=pl.BlockSpec((seq, hb), lambda b, h: (b, h)),
        out_shape=jax.ShapeDtypeStruct((batch * seq, MLSTM_WIDTH), jnp.bfloat16),
        compiler_params=_cparams(("parallel", "parallel")),
        name="mlstm",
    )(z, z, z, z, zs, conv_qk, conv_qk, gain.reshape(1, MLSTM_WIDTH))


def _retention_kernel(q_ref, k_ref, v_ref, g_ref, cos_ref, sin_ref, lg_ref, gain_ref, o_ref, r_ref, decay_ref,
                      *, n_chunks):
    L = CHUNK
    H = RET_HEADS
    D = HEAD_DIM
    bf = jnp.bfloat16
    row = lax.broadcasted_iota(jnp.int32, (L, L), 0)
    col = lax.broadcasted_iota(jnp.int32, (L, L), 1)
    rel = (row - col).astype(jnp.float32)
    idx = lax.broadcasted_iota(jnp.int32, (L, 1), 0).astype(jnp.float32)
    gain = gain_ref[...]
    scale = HEAD_DIM ** -0.5
    lgs = [lg_ref[h][:, 0:1] for h in range(H)]
    for h in range(H):
        decay_ref[h] = jnp.where(rel >= 0, jnp.exp(lgs[h] * jnp.maximum(rel, 0.0)), 0.0)
    r_ref[...] = jnp.zeros(r_ref.shape, jnp.float32)

    def body(c, carry):
        start = pl.multiple_of(c * L, L)
        cosf = cos_ref[pl.ds(start, L), :]
        sinf = sin_ref[pl.ds(start, L), :]
        q_all = q_ref[pl.ds(start, L), :]
        k_all = k_ref[pl.ds(start, L), :]
        v_all = v_ref[pl.ds(start, L), :]
        heads = []
        for h in range(H):
            sl = slice(h * D, (h + 1) * D)

            def rot(x):
                return x * cosf + pltpu.roll(x, D // 2, 1) * sinf

            k = rot(k_all[:, sl]) * scale
            heads.append(dict(sl=sl, qb=rot(q_all[:, sl]).astype(bf), kb=k.astype(bf), vb=v_all[:, sl].astype(bf),
                              kz=k * jnp.exp(lgs[h] * (L - 1.0 - idx))))
        for h, s in enumerate(heads):
            s["qk"] = lax.dot_general(s["qb"], s["kb"], _NT, preferred_element_type=jnp.float32)
            s["qr"] = jnp.dot(s["qb"], r_ref[h].astype(bf), preferred_element_type=jnp.float32)
            s["kv"] = jnp.dot(s["kz"].T.astype(bf), s["vb"], preferred_element_type=jnp.float32)
        for h, s in enumerate(heads):
            s["sv"] = jnp.dot((s["qk"] * decay_ref[h]).astype(bf), s["vb"], preferred_element_type=jnp.float32)
        outs = []
        for h, s in enumerate(heads):
            out = s["sv"] + jnp.exp(lgs[h] * (idx + 1.0)) * s["qr"]
            outs.append(_head_norm(out, gain[:, s["sl"]]))
            r_ref[h] = jnp.exp(lgs[h] * L) * r_ref[h] + s["kv"]
        gate = g_ref[pl.ds(start, L), :]
        o_ref[pl.ds(start, L), :] = (gate * _sigmoid(gate) * jnp.concatenate(outs, axis=1)).astype(o_ref.dtype)
        return carry

    lax.fori_loop(0, n_chunks, body, 0)


def _retention(z, cosf, sinf, lg, gain, batch, seq):
    nh = RET_HEADS
    hb = HEAD_DIM
    width = RET_WIDTH

    def seg(off):
        assert off % width == 0
        return pl.BlockSpec((seq, width), lambda b, off=off: (b, off // width))

    table = pl.BlockSpec((seq, hb), lambda b: (0, 0))
    return pl.pallas_call(
        functools.partial(_retention_kernel, n_chunks=seq // CHUNK),
        grid=(batch,),
        in_specs=[seg(OFF_RQ), seg(OFF_RK), seg(OFF_RV), seg(OFF_RG), table, table,
                  pl.BlockSpec((nh, 1, hb), lambda b: (0, 0, 0)),
                  pl.BlockSpec((1, width), lambda b: (0, 0))],
        out_specs=pl.BlockSpec((seq, width), lambda b: (b, 0)),
        out_shape=jax.ShapeDtypeStruct((batch * seq, width), jnp.bfloat16),
        scratch_shapes=[pltpu.VMEM((nh, hb, hb), jnp.float32),
                        pltpu.VMEM((nh, CHUNK, CHUNK), jnp.float32)],
        compiler_params=_cparams(("parallel",)),
        name="retention",
    )(z, z, z, z, cosf, sinf, lg, gain.reshape(1, width))


def _gelu_tanh(x):
    return 0.5 * x * (1.0 + jnp.tanh(math.sqrt(2.0 / math.pi) * (x + 0.044715 * (x * x * x))))


def _compress_kernel(k_ref, v_ref, pk_ref, w1k_ref, w2k_ref, pv_ref, w1v_ref, w2v_ref, ko_ref, vo_ref, pad_ref,
                     *, seq):
    nblk = seq // CMP_STRIDE
    for x_ref, p_ref, w1_ref, w2_ref, o_ref, transposed in ((k_ref, pk_ref, w1k_ref, w2k_ref, ko_ref, False),
                                                             (v_ref, pv_ref, w1v_ref, w2v_ref, vo_ref, True)):
        pad_ref[0:seq, :] = x_ref[...]
        pad_ref[seq:seq + CMP_BLOCK, :] = jnp.zeros((CMP_BLOCK, HEAD_DIM), jnp.float32)
        acc = jnp.zeros((nblk, HEAD_DIM), jnp.float32)
        for p in range(CMP_BLOCK):
            xs = pad_ref[pl.ds(p, nblk, stride=CMP_STRIDE), :] + p_ref[p:p + 1, :]
            acc = acc + jnp.dot(xs.astype(jnp.bfloat16), w1_ref[p * HEAD_DIM:(p + 1) * HEAD_DIM, :],
                                preferred_element_type=jnp.float32)
        mid = _gelu_tanh(acc).astype(jnp.bfloat16)
        if transposed:
            out = lax.dot_general(w2_ref[...], mid, _NT, preferred_element_type=jnp.float32)
        else:
            out = jnp.dot(mid, w2_ref[...], preferred_element_type=jnp.float32)
        o_ref[0, 0] = out.astype(o_ref.dtype)


def _compress(z, pos_k, w1_k, w2_k, pos_v, w1_v, w2_v, batch, seq):
    hb = HEAD_DIM
    nblk = seq // CMP_STRIDE
    cmp_in = CMP_BLOCK * hb

    def seg(off):
        return pl.BlockSpec((seq, hb), lambda b, g, off=off: (b, off // hb + g))

    def full(shape):
        return pl.BlockSpec(shape, lambda b, g: (0,) * len(shape))

    out_specs = [pl.BlockSpec((1, 1, nblk, hb), lambda b, g: (b, g, 0, 0)),
                 pl.BlockSpec((1, 1, hb, nblk), lambda b, g: (b, g, 0, 0))]
    out_shape = [jax.ShapeDtypeStruct((batch, NSA_KV_GROUPS, nblk, hb), jnp.bfloat16),
                 jax.ShapeDtypeStruct((batch, NSA_KV_GROUPS, hb, nblk), jnp.bfloat16)]
    return pl.pallas_call(
        functools.partial(_compress_kernel, seq=seq),
        grid=(batch, NSA_KV_GROUPS),
        in_specs=[seg(OFF_NKC), seg(OFF_NVC),
                  full((CMP_BLOCK, hb)), full((cmp_in, hb)), full((hb, hb)),
                  full((CMP_BLOCK, hb)), full((cmp_in, hb)), full((hb, hb))],
        out_specs=out_specs,
        out_shape=out_shape,
        scratch_shapes=[pltpu.VMEM((seq + CMP_BLOCK, hb), jnp.float32)],
        compiler_params=_cparams(("parallel", "parallel")),
        name="nsa_compress",
    )(z, z, pos_k, w1_k, w2_k, pos_v, w1_v, w2_v)


def _rel_bucket_np(dist):
    n = np.maximum(dist, 0)
    exact = REL_BUCKETS // 2
    log_ratio = (np.log(np.maximum(n, 1).astype(np.float32) / np.float32(exact))
                 / np.float32(math.log(REL_MAX_DIST / exact)))
    large = np.minimum(exact + (log_ratio * np.float32(REL_BUCKETS - exact)).astype(np.int32), REL_BUCKETS - 1)
    return np.where(n < exact, n, large).astype(np.int32)


def _bias_kernel(tbl_ref, idxc_ref, idxt_ref, bc_ref, bt_ref):
    h = pl.program_id(0)
    idxc = idxc_ref[...]
    idxt = idxt_ref[...]
    accc = jnp.zeros(idxc.shape, jnp.float32)
    acct = jnp.zeros(idxt.shape, jnp.float32)
    for b in range(REL_BUCKETS):
        val = tbl_ref[b, h]
        accc = jnp.where(idxc == b, val, accc)
        acct = jnp.where(idxt == b, val, acct)
    bc_ref[0] = accc
    bt_ref[:, 0] = acct


def _bias_tables(rel_bias, seq):
    nblk = seq // CMP_STRIDE
    j = np.arange(nblk)[:, None]
    t = np.arange(seq)[None, :]
    idx_c = _rel_bucket_np(t - (j * CMP_STRIDE + CMP_BLOCK - 1))
    kk = np.arange(TILE)[:, None]
    i = np.arange(TILE)[None, :]
    idx_t = np.stack([_rel_bucket_np(i - kk + d * TILE) for d in range(3)])
    return pl.pallas_call(
        _bias_kernel,
        grid=(NSA_HEADS,),
        in_specs=[pl.BlockSpec(memory_space=pltpu.SMEM),
                  pl.BlockSpec((nblk, seq), lambda h: (0, 0)),
                  pl.BlockSpec((3, TILE, TILE), lambda h: (0, 0, 0))],
        out_specs=[pl.BlockSpec((1, nblk, seq), lambda h: (h, 0, 0)),
                   pl.BlockSpec((3, 1, TILE, TILE), lambda h: (0, h, 0, 0))],
        out_shape=[jax.ShapeDtypeStruct((NSA_HEADS, nblk, seq), jnp.float32),
                   jax.ShapeDtypeStruct((3, NSA_HEADS, TILE, TILE), jnp.float32)],
        compiler_params=_cparams(("parallel",)),
        name="nsa_bias_tables",
    )(rel_bias, jnp.asarray(idx_c), jnp.asarray(idx_t))


def _nsa_kernel(q_ref, kc_ref, vct_ref, ks_ref, vs_ref, kw_ref, vw_ref, sm_ref, bc_ref, bt_ref, o_ref,
                kb_ref, vt_ref, acc_ref, sw_ref, ss_ref, sel_ref, gate_ref, *, seq):
    R = NSA_GROUP_SIZE
    T = TILE
    nblk = seq // CMP_STRIDE
    n_sel = seq // SEL_BLOCK
    n_tiles = seq // T
    top = min(SEL_TOPK, n_sel)
    bf = jnp.bfloat16
    g = pl.program_id(1)
    n = pl.program_id(2)
    t0 = n * T

    @pl.when(n == 0)
    def _():
        for slot, (k_ref, v_ref) in enumerate(((ks_ref, vs_ref), (kw_ref, vw_ref))):
            for c in range(n_tiles):
                kb_ref[slot, c] = k_ref[c * T:(c + 1) * T, :].astype(bf)
                vt_ref[slot, c] = v_ref[c * T:(c + 1) * T, :].T.astype(bf)

    q = q_ref[...] * (HEAD_DIM ** -0.5)
    qt = jnp.concatenate([q[:, r * HEAD_DIM:(r + 1) * HEAD_DIM].T for r in range(R)], axis=1).astype(bf)

    def rep(x):
        return jnp.concatenate([x] * R, axis=1)

    j_c = lax.broadcasted_iota(jnp.int32, (nblk, T), 0)
    t_c = t0 + lax.broadcasted_iota(jnp.int32, (nblk, T), 1)
    valid_c = rep(t_c >= j_c * CMP_STRIDE + (CMP_BLOCK - 1))
    s_c = (jnp.dot(kc_ref[0, 0], qt, preferred_element_type=jnp.float32)
           + jnp.concatenate([bc_ref[r] for r in range(R)], axis=1))
    s_c = jnp.where(valid_c, s_c, NEG_INF)
    e_c = jnp.exp(s_c - jnp.max(s_c, axis=0, keepdims=True))
    p_c = jnp.where(valid_c, e_c * (1.0 / jnp.sum(e_c, axis=0, keepdims=True)), 0.0)
    o_c = jnp.dot(vct_ref[0, 0], p_c.astype(bf), preferred_element_type=jnp.float32)

    p_sum = p_c[:, 0:T]
    for r in range(1, R):
        p_sum = p_sum + p_c[:, r * T:(r + 1) * T]
    ov_s = lax.broadcasted_iota(jnp.int32, (T, nblk), 0)
    ov_j = lax.broadcasted_iota(jnp.int32, (T, nblk), 1)
    ratio = SEL_BLOCK // CMP_STRIDE
    span = CMP_BLOCK // CMP_STRIDE
    overlap = ((ov_j < ratio * ov_s + ratio) & (ov_j + span > ratio * ov_s) & (ov_s < n_sel)).astype(jnp.float32)
    imp = jnp.dot(overlap, p_sum, preferred_element_type=jnp.float32, precision=lax.Precision.HIGHEST)

    rows = lax.broadcasted_iota(jnp.int32, (T, T), 0)
    t_q = t0 + lax.broadcasted_iota(jnp.int32, (T, T), 1)
    cur = t_q // SEL_BLOCK
    forced = (rows == 0) | (rows == cur) | (rows == cur - 1)
    val = jnp.where(forced, jnp.inf, jnp.where((rows > cur) | (rows >= n_sel), -jnp.inf, imp))
    blk_f = rows.astype(jnp.float32)
    sel = jnp.zeros((T, T), jnp.float32)
    for _ in range(top):
        best = jnp.max(val, axis=0, keepdims=True)
        first = jnp.min(jnp.where(val == best, blk_f, float(T)), axis=0, keepdims=True)
        hit = blk_f == first
        sel = jnp.where(hit, 1.0, sel)
        val = jnp.where(hit, -jnp.inf, val)
    sel_ref[...] = sel

    def scores(kv, c, allowed):
        mask_add = jnp.where(allowed, 0.0, NEG_INF)
        c_read = jnp.clip(c, 0, n)
        tile_dist = jnp.minimum(n - c_read, 2)
        bias = jnp.concatenate([bt_ref[tile_dist, r] + mask_add for r in range(R)], axis=1)
        return jnp.dot(kb_ref[kv, c_read], qt, preferred_element_type=jnp.float32) + bias

    def col_max(tiles):
        out = jnp.max(tiles[0], axis=0, keepdims=True)
        for s in tiles[1:]:
            out = jnp.maximum(out, jnp.max(s, axis=0, keepdims=True))
        return out

    def weighted_values(kv, tile_ids, probs):
        vt = jnp.concatenate([vt_ref[kv, jnp.clip(c, 0, n)] for c in tile_ids], axis=1)
        total = probs[0].sum(axis=0, keepdims=True)
        for p in probs[1:]:
            total = total + p.sum(axis=0, keepdims=True)
        pv = jnp.dot(vt, jnp.concatenate(probs, axis=0).astype(bf), preferred_element_type=jnp.float32)
        return pv, total

    n_win = WINDOW // T + 1
    win_ids = [n - (n_win - 1) + w for w in range(n_win)]
    for w, c in enumerate(win_ids):
        kpos = c * T + rows
        dist = t_q - kpos
        sw_ref[w] = scores(1, c, (dist >= 0) & (dist < WINDOW) & (kpos >= 0))
    m_w = col_max([sw_ref[w] for w in range(n_win)])
    acc_w, l_w = weighted_values(1, win_ids, [jnp.exp(sw_ref[w] - m_w) for w in range(n_win)])
    o_w = acc_w * (1.0 / l_w)

    blocks_per_tile = T // SEL_BLOCK
    trips = (n + SEL_GROUP) // SEL_GROUP

    def sel_scores(c):
        kpos = c * T + rows
        chosen = sel_ref[pl.ds(c * blocks_per_tile, 1), :]
        for i in range(1, blocks_per_tile):
            chosen = jnp.where(rows < i * SEL_BLOCK, chosen, sel_ref[pl.ds(c * blocks_per_tile + i, 1), :])
        return scores(0, c, (chosen > 0.5) & (kpos <= t_q))

    def sel_pass1(i, m_run):
        ids = [SEL_GROUP * i + u for u in range(SEL_GROUP)]
        tiles = [sel_scores(c) for c in ids]
        for c, s in zip(ids, tiles):
            ss_ref[c] = s
        return jnp.maximum(m_run, col_max(tiles))

    m_s = lax.fori_loop(0, trips, sel_pass1, jnp.full((1, R * T), NEG_INF, jnp.float32))
    acc_ref[...] = jnp.zeros(acc_ref.shape, jnp.float32)

    def sel_pass2(i, l_run):
        ids = [SEL_GROUP * i + u for u in range(SEL_GROUP)]
        pv, total = weighted_values(0, ids, [jnp.exp(ss_ref[c] - m_s) for c in ids])
        acc_ref[...] += pv
        return l_run + total

    l_s = lax.fori_loop(0, trips, sel_pass2, jnp.zeros((1, R * T), jnp.float32))
    o_s = acc_ref[...] * (1.0 / l_s)

    gate_ref[...] = _sigmoid(sm_ref[...]).T
    outs = []
    for r in range(R):
        def gate(branch):
            return gate_ref[pl.ds(SM_GATE + branch * NSA_HEADS + g * R + r, 1), :]
        sl = slice(r * T, (r + 1) * T)
        outs.append((gate(0) * o_c[:, sl] + gate(1) * o_s[:, sl] + gate(2) * o_w[:, sl]).T)
    o_ref[...] = jnp.concatenate(outs, axis=1).astype(o_ref.dtype)


def _nsa(z, zs, k_c, v_c, bias_c, bias_t, batch, seq):
    hb = HEAD_DIM
    R = NSA_GROUP_SIZE
    nblk = seq // CMP_STRIDE
    nq = seq // TILE
    gw = R * hb
    assert OFF_NQ % gw == 0

    def seg(off):
        return pl.BlockSpec((seq, hb), lambda b, g, n, off=off: (b, off // hb + g))

    return pl.pallas_call(
        functools.partial(_nsa_kernel, seq=seq),
        grid=(batch, NSA_KV_GROUPS, nq),
        in_specs=[pl.BlockSpec((TILE, gw), lambda b, g, n: (b * nq + n, OFF_NQ // gw + g)),
                  pl.BlockSpec((1, 1, nblk, hb), lambda b, g, n: (b, g, 0, 0)),
                  pl.BlockSpec((1, 1, hb, nblk), lambda b, g, n: (b, g, 0, 0)),
                  seg(OFF_NKS), seg(OFF_NVS), seg(OFF_NKW), seg(OFF_NVW),
                  pl.BlockSpec((TILE, SMALL_COLS), lambda b, g, n: (b * nq + n, 0)),
                  pl.BlockSpec((R, nblk, TILE), lambda b, g, n: (g, 0, n)),
                  pl.BlockSpec((3, R, TILE, TILE), lambda b, g, n: (0, g, 0, 0))],
        out_specs=pl.BlockSpec((TILE, gw), lambda b, g, n: (b * nq + n, g)),
        out_shape=jax.ShapeDtypeStruct((batch * seq, NSA_WIDTH), jnp.bfloat16),
        scratch_shapes=[pltpu.VMEM((2, nq, TILE, hb), jnp.bfloat16),
                        pltpu.VMEM((2, nq, hb, TILE), jnp.bfloat16),
                        pltpu.VMEM((hb, R * TILE), jnp.float32),
                        pltpu.VMEM((WINDOW // TILE + 1, TILE, R * TILE), jnp.float32),
                        pltpu.VMEM((nq, TILE, R * TILE), jnp.float32),
                        pltpu.VMEM((TILE, TILE), jnp.float32),
                        pltpu.VMEM((SMALL_COLS, TILE), jnp.float32)],
        compiler_params=_cparams(("parallel", "parallel", "arbitrary")),
        name="nsa_attention",
    )(z, k_c, v_c, z, z, z, z, zs, bias_c, bias_t)


def _rope_tables(seq):
    half = HEAD_DIM // 2
    inv_freq = 1.0 / (10000.0 ** jnp.linspace(0.0, 1.0, half))
    ang = jnp.arange(seq).astype(jnp.float32)[:, None] * inv_freq[None, :]
    cos, sin = jnp.cos(ang), jnp.sin(ang)
    return jnp.concatenate([cos, cos], axis=-1), jnp.concatenate([-sin, sin], axis=-1)


def _retention_log_decay():
    lg = np.log1p(-np.exp2(-5.0 - np.arange(RET_HEADS, dtype=np.float64))).astype(np.float32)
    return jnp.asarray(np.broadcast_to(lg[:, None, None], (RET_HEADS, 1, HEAD_DIM)).copy())


def _split_in_proj(w_in, b_in, w_gate_down):
    bf = jnp.bfloat16
    a0 = 4 * MLSTM_WIDTH
    a1 = a0 + 2 * MLSTM_HEADS
    a2 = a1 + NSA_WIDTH + 6 * NSA_KV_WIDTH
    a3 = a2 + 3 * NSA_HEADS
    depth, d, _ = w_in.shape
    w_main = jnp.concatenate([w_gate_down.astype(bf), w_in[:, :, :a0].astype(bf), w_in[:, :, a1:a2].astype(bf),
                              w_in[:, :, a3:].astype(bf)], axis=2)
    b_main = jnp.concatenate([jnp.zeros((depth, GATE_RANK), jnp.float32), b_in[:, :a0], b_in[:, a1:a2],
                              b_in[:, a3:]], axis=1)
    n_small = (a1 - a0) + (a3 - a2)
    w_small = jnp.concatenate([w_in[:, :, a0:a1].astype(bf), w_in[:, :, a2:a3].astype(bf),
                               jnp.zeros((depth, d, SMALL_COLS - n_small), bf)], axis=2)
    b_small = jnp.concatenate([b_in[:, a0:a1], b_in[:, a2:a3],
                               jnp.zeros((depth, SMALL_COLS - n_small), jnp.float32)], axis=1)
    return w_main, b_main, w_small, b_small


def kernel(x, rel_bias, norm_mix_g, w_in, b_in, conv_qk, mlstm_norm_g, cmp_pos_k, cmp_w1_k, cmp_w2_k, cmp_pos_v, cmp_w1_v, cmp_w2_v, ret_norm_g, w_br_mlstm, w_br_nsa, w_br_ret, w_gate_down, w_gate_up, b_gate, w_out, norm_mlp_g, w_up, w_down, final_norm_g):
    batch, seq, d = x.shape
    depth = w_in.shape[0]
    bf = jnp.bfloat16
    f32 = jnp.float32
    bias_c, bias_t = _bias_tables(rel_bias, seq)
    cosf, sinf = _rope_tables(seq)
    lg = _retention_log_decay()
    w_main, b_main, w_small, b_small = _split_in_proj(w_in, b_in, w_gate_down)
    w_down_bf = w_down.astype(bf)
    w1_k, w2_k, w1_v, w2_vt = cmp_w1_k.astype(bf), cmp_w2_k.astype(bf), cmp_w1_v.astype(bf), jnp.swapaxes(cmp_w2_v, 1, 2).astype(bf)
    xr = x.reshape(batch * seq, d)
    for l in range(depth):
        h = _rmsnorm(xr, norm_mix_g[l], bf)
        z = _matmul(h, w_main, l, bias=b_main, out_dtype=f32, tm=1024, tn=512, tk=D_MODEL, name="in_proj")
        zs = _matmul(h, w_small, l, bias=b_small, out_dtype=f32, tm=1024, tn=SMALL_COLS, tk=D_MODEL,
                     name="in_proj_small")
        y_a = _mlstm(z, zs, conv_qk[l], mlstm_norm_g[l], batch, seq)
        k_c, v_c = _compress(z, cmp_pos_k[l], w1_k[l], w2_k[l], cmp_pos_v[l], w1_v[l], w2_vt[l], batch, seq)
        y_b = _nsa(z, zs, k_c, v_c, bias_c, bias_t, batch, seq)
        y_c = _retention(z, cosf, sinf, lg, ret_norm_g[l], batch, seq)
        merged = _merge(z, y_a, y_b, y_c, l, w_gate_up, b_gate, w_br_mlstm, w_br_nsa, w_br_ret)
        xr = _matmul(merged, w_out, l, res=xr, out_dtype=f32, tm=1024, tn=512, tk=D_MODEL, name="out_proj")
        h = _rmsnorm(xr, norm_mlp_g[l], bf)
        up = _matmul(h, w_up, l, act="relu2", out_dtype=bf, tm=1024, tn=512, tk=D_MODEL, name="mlp_up")
        xr = _matmul(up, w_down_bf, l, res=xr, out_dtype=f32, tm=1024, tn=512, tk=4096, name="mlp_down")
    return _rmsnorm(xr, final_norm_g, f32).reshape(batch, seq, d)
```

```python
import functools
import math

import numpy as np
import jax
import jax.numpy as jnp
from jax import lax
from jax.experimental import pallas as pl
from jax.experimental.pallas import tpu as pltpu

D_MODEL = 4096
HEAD_DIM = 128
MIX_WIDTH = D_MODEL // 2
MLSTM_HEADS = MIX_WIDTH // (4 * HEAD_DIM)
NSA_HEADS = MIX_WIDTH // (2 * HEAD_DIM)
RET_HEADS = MIX_WIDTH // (4 * HEAD_DIM)
NSA_KV_GROUPS = 2
NSA_GROUP_SIZE = NSA_HEADS // NSA_KV_GROUPS
MLSTM_WIDTH = MLSTM_HEADS * HEAD_DIM
NSA_WIDTH = NSA_HEADS * HEAD_DIM
RET_WIDTH = RET_HEADS * HEAD_DIM
NSA_KV_WIDTH = NSA_KV_GROUPS * HEAD_DIM
CONV_WIDTH = 4
CMP_BLOCK = 32
CMP_STRIDE = 16
SEL_BLOCK = 64
SEL_TOPK = 8
WINDOW = 512
REL_BUCKETS = 32
REL_MAX_DIST = 128
GATE_RANK = D_MODEL // 4
N_BRANCH = 3
EPS = 1e-6
NEG_INF = -1e30

CHUNK = 128
TILE = 128
SEL_GROUP = 4

OFF_GL = 0
OFF_MQ = OFF_GL + GATE_RANK
OFF_MK = OFF_MQ + MLSTM_WIDTH
OFF_MV = OFF_MK + MLSTM_WIDTH
OFF_MO = OFF_MV + MLSTM_WIDTH
OFF_NQ = OFF_MO + MLSTM_WIDTH
OFF_NKC = OFF_NQ + NSA_WIDTH
OFF_NVC = OFF_NKC + NSA_KV_WIDTH
OFF_NKS = OFF_NVC + NSA_KV_WIDTH
OFF_NVS = OFF_NKS + NSA_KV_WIDTH
OFF_NKW = OFF_NVS + NSA_KV_WIDTH
OFF_NVW = OFF_NKW + NSA_KV_WIDTH
OFF_RQ = OFF_NVW + NSA_KV_WIDTH
OFF_RK = OFF_RQ + RET_WIDTH
OFF_RV = OFF_RK + RET_WIDTH
OFF_RG = OFF_RV + RET_WIDTH
MAIN_COLS = OFF_RG + RET_WIDTH
SMALL_COLS = 128
SM_I = 0
SM_F = MLSTM_HEADS
SM_GATE = 2 * MLSTM_HEADS

VMEM_LIMIT = 56 * 1024 * 1024

_NT = (((1,), (1,)), ((), ()))


def _sigmoid(x):
    return 1.0 / (1.0 + jnp.exp(-x))


def _cparams(sem):
    return pltpu.CompilerParams(dimension_semantics=sem, vmem_limit_bytes=VMEM_LIMIT)


def _rmsnorm_kernel(x_ref, g_ref, o_ref):
    x = x_ref[...]
    ms = jnp.mean(x * x, axis=-1, keepdims=True)
    o_ref[...] = (x * lax.rsqrt(ms + EPS) * g_ref[...]).astype(o_ref.dtype)


def _rmsnorm(x, gain, out_dtype, tm=256):
    t, d = x.shape
    return pl.pallas_call(
        _rmsnorm_kernel,
        grid=(t // tm,),
        in_specs=[pl.BlockSpec((tm, d), lambda i: (i, 0)),
                  pl.BlockSpec((1, d), lambda i: (0, 0))],
        out_specs=pl.BlockSpec((tm, d), lambda i: (i, 0)),
        out_shape=jax.ShapeDtypeStruct((t, d), out_dtype),
        compiler_params=_cparams(("parallel",)),
        name="rmsnorm",
    )(x, gain.reshape(1, d))


def _mm_kernel(*refs, nk, act, has_bias, has_res, cast_w):
    a_ref, w_ref = refs[0], refs[1]
    pos = 2
    b_ref = refs[pos] if has_bias else None
    pos += int(has_bias)
    r_ref = refs[pos] if has_res else None
    pos += int(has_res)
    o_ref = refs[pos]
    pos += 1
    acc_ref = refs[pos] if nk > 1 else None
    pos += int(nk > 1)

    if cast_w:
        wb_ref = refs[pos]

        @pl.when(pl.program_id(1) == 0)
        def _():
            wb_ref[...] = w_ref[...].astype(jnp.bfloat16)

        w_ref = wb_ref

    part = jnp.dot(a_ref[...], w_ref[...], preferred_element_type=jnp.float32)

    def finish(acc):
        if has_bias:
            acc = acc + b_ref[...]
        if act == "relu2":
            acc = jnp.square(jnp.maximum(acc, 0.0))
        if has_res:
            acc = acc + r_ref[...]
        o_ref[...] = acc.astype(o_ref.dtype)

    if nk == 1:
        finish(part)
    else:
        k = pl.program_id(2)

        @pl.when(k == 0)
        def _():
            acc_ref[...] = part

        @pl.when(k > 0)
        def _():
            acc_ref[...] += part

        @pl.when(k == nk - 1)
        def _():
            finish(acc_ref[...])


def _matmul(a, w, layer, *, bias=None, res=None, act=None, out_dtype, tm, tn, tk, name):
    m, kdim = a.shape
    n = w.shape[2]
    tm, tn, tk = min(tm, m), min(tn, n), min(tk, kdim)
    nk = kdim // tk
    cast_w = w.dtype != jnp.bfloat16
    assert not (cast_w and nk > 1)
    in_specs = [pl.BlockSpec((tm, tk), lambda j, i, k: (i, k)),
                pl.BlockSpec((None, tk, tn), lambda j, i, k: (layer, k, j))]
    args = [a, w]
    if bias is not None:
        in_specs.append(pl.BlockSpec((None, 1, tn), lambda j, i, k: (layer, 0, j)))
        args.append(bias.reshape(bias.shape[0], 1, n))
    if res is not None:
        in_specs.append(pl.BlockSpec((tm, tn), lambda j, i, k: (i, j)))
        args.append(res)
    scratch = [pltpu.VMEM((tm, tn), jnp.float32)] if nk > 1 else []
    if cast_w:
        scratch.append(pltpu.VMEM((tk, tn), jnp.bfloat16))
    return pl.pallas_call(
        functools.partial(_mm_kernel, nk=nk, act=act, has_bias=bias is not None, has_res=res is not None,
                          cast_w=cast_w),
        grid=(n // tn, m // tm, nk),
        in_specs=in_specs,
        out_specs=pl.BlockSpec((tm, tn), lambda j, i, k: (i, j)),
        out_shape=jax.ShapeDtypeStruct((m, n), out_dtype),
        scratch_shapes=scratch,
        compiler_params=_cparams(("parallel", "arbitrary", "arbitrary")),
        name=name,
    )(*args)


def _merge_kernel(gl_ref, ya_ref, yb_ref, yc_ref, wg0_ref, wg1_ref, wg2_ref, bg0_ref, bg1_ref, bg2_ref,
                  wa_ref, wb_ref, wc_ref, o_ref, *wbf_refs):
    @pl.when(pl.program_id(1) == 0)
    def _():
        for src, dst in zip((wg0_ref, wg1_ref, wg2_ref, wa_ref, wb_ref, wc_ref), wbf_refs):
            dst[...] = src[...].astype(jnp.bfloat16)

    wg0, wg1, wg2, wa, wb, wc = wbf_refs
    gl = gl_ref[...].astype(jnp.bfloat16)

    def branch(wg, bg_ref, y_ref, w):
        gate = _sigmoid(jnp.dot(gl, wg[...], preferred_element_type=jnp.float32) + bg_ref[...])
        return gate * jnp.dot(y_ref[...], w[...], preferred_element_type=jnp.float32)

    out = branch(wg0, bg0_ref, ya_ref, wa) + branch(wg1, bg1_ref, yb_ref, wb) + branch(wg2, bg2_ref, yc_ref, wc)
    o_ref[...] = out.astype(o_ref.dtype)


def _merge(z, ya, yb, yc, layer, w_gate_up, b_gate, wa, wb, wc, tm=1024, tn=512):
    t = z.shape[0]
    d = wa.shape[2]
    tm = min(tm, t)
    nj = d // tn
    assert OFF_GL % GATE_RANK == 0
    gl_blk = OFF_GL // GATE_RANK

    def wg_spec(i):
        return pl.BlockSpec((None, GATE_RANK, tn), lambda j, m, i=i: (layer, 0, i * nj + j))

    def bg_spec(i):
        return pl.BlockSpec((None, 1, tn), lambda j, m, i=i: (layer, 0, i * nj + j))

    def row_spec(width):
        return pl.BlockSpec((tm, width), lambda j, m: (m, 0))

    def w_spec(width):
        return pl.BlockSpec((None, width, tn), lambda j, m: (layer, 0, j))

    bg = b_gate.reshape(b_gate.shape[0], 1, N_BRANCH * d)
    bf_scratch = [pltpu.VMEM((rows, tn), jnp.bfloat16)
                  for rows in (GATE_RANK, GATE_RANK, GATE_RANK, MLSTM_WIDTH, NSA_WIDTH, RET_WIDTH)]
    return pl.pallas_call(
        _merge_kernel,
        grid=(nj, t // tm),
        in_specs=[pl.BlockSpec((tm, GATE_RANK), lambda j, m: (m, gl_blk)),
                  row_spec(MLSTM_WIDTH), row_spec(NSA_WIDTH), row_spec(RET_WIDTH),
                  wg_spec(0), wg_spec(1), wg_spec(2), bg_spec(0), bg_spec(1), bg_spec(2),
                  w_spec(MLSTM_WIDTH), w_spec(NSA_WIDTH), w_spec(RET_WIDTH)],
        out_specs=pl.BlockSpec((tm, tn), lambda j, m: (m, j)),
        out_shape=jax.ShapeDtypeStruct((t, d), jnp.bfloat16),
        scratch_shapes=bf_scratch,
        compiler_params=_cparams(("parallel", "arbitrary")),
        name="gated_merge",
    )(z, ya, yb, yc, w_gate_up, w_gate_up, w_gate_up, bg, bg, bg, wa, wb, wc)


def _head_norm(h, gain):
    mu = jnp.mean(h, axis=-1, keepdims=True)
    d = h - mu
    var = jnp.mean(d * d, axis=-1, keepdims=True)
    return d * lax.rsqrt(var + EPS) * gain


def _mlstm_kernel(q_ref, k_ref, v_ref, og_ref, sm_ref, cq_ref, ck_ref, gain_ref, o_ref, *, n_chunks):
    L = CHUNK
    h = pl.program_id(1)
    lane_ids = lax.broadcasted_iota(jnp.int32, (L, SMALL_COLS), 1)
    row = lax.broadcasted_iota(jnp.int32, (L, L), 0)
    col = lax.broadcasted_iota(jnp.int32, (L, L), 1)
    causal = col <= row
    eye = col == row
    cq = cq_ref[...]
    ck = ck_ref[...]
    gain = gain_ref[...]
    scale = HEAD_DIM ** -0.5

    def conv_silu(ref, w, c, start):
        cur = ref[pl.ds(start, L), :]
        pstart = pl.multiple_of(jnp.maximum(start - 8, 0), 8)
        prev = jnp.where(c > 0, ref[pl.ds(pstart, 8), :], 0.0)
        ext = jnp.concatenate([prev, cur], axis=0)
        y = w[CONV_WIDTH - 1:CONV_WIDTH, :] * cur
        for kk in range(CONV_WIDTH - 1):
            off = 8 - (CONV_WIDTH - 1) + kk
            y = y + w[kk:kk + 1, :] * ext[off:off + L, :]
        return y * _sigmoid(y)

    def lane_pick(x, lane):
        return jnp.sum(jnp.where(lane_ids == lane, x, 0.0), axis=1, keepdims=True)

    def body(c, carry):
        cmat, nvec, m = carry
        start = pl.multiple_of(c * L, L)
        q = conv_silu(q_ref, cq, c, start)
        k = conv_silu(k_ref, ck, c, start) * scale
        v = v_ref[pl.ds(start, L), :]
        sm = sm_ref[pl.ds(start, L), :]
        i_col = lane_pick(sm, SM_I + h)
        f_col = lane_pick(sm, SM_F + h)
        ls_col = jnp.minimum(f_col, 0.0) - jnp.log(1.0 + jnp.exp(-jnp.abs(f_col)))
        a_row = jnp.sum(jnp.where(row <= col, ls_col, 0.0), axis=0, keepdims=True)
        a_col = jnp.sum(jnp.where(eye, a_row, 0.0), axis=1, keepdims=True)
        i_row = jnp.sum(jnp.where(eye, i_col, 0.0), axis=0, keepdims=True)
        g = jnp.sum(ls_col, axis=0, keepdims=True)
        w_col = g - a_col + i_col
        m_loc = jnp.max(w_col, axis=0, keepdims=True)
        e_col = jnp.exp(w_col - m_loc)

        log_d = jnp.where(causal, a_col - a_row + i_row, NEG_INF)
        log_inter = a_col + m
        m_row = jnp.maximum(log_inter, jnp.max(log_d, axis=1, keepdims=True))
        inter = jnp.exp(log_inter - m_row)
        dmat = jnp.exp(log_d - m_row)

        qb = q.astype(jnp.bfloat16)
        kb = k.astype(jnp.bfloat16)
        vb = v.astype(jnp.bfloat16)
        sc = lax.dot_general(qb, kb, _NT, preferred_element_type=jnp.float32) * dmat
        num = (inter * jnp.dot(qb, cmat.astype(jnp.bfloat16), preferred_element_type=jnp.float32)
               + jnp.dot(sc.astype(jnp.bfloat16), vb, preferred_element_type=jnp.float32))
        den = inter * jnp.sum(q * nvec, axis=1, keepdims=True) + jnp.sum(sc, axis=1, keepdims=True)
        hout = num / jnp.maximum(jnp.abs(den), jnp.exp(-m_row))

        y = _head_norm(hout, gain)
        o_ref[pl.ds(start, L), :] = (_sigmoid(og_ref[pl.ds(start, L), :]) * y).astype(o_ref.dtype)

        m_new = jnp.maximum(g + m, m_loc)
        a_old = jnp.exp(g + m - m_new)
        a_new = jnp.exp(m_loc - m_new)
        ke = k * e_col
        c_new = a_old * cmat + a_new * jnp.dot(ke.T.astype(jnp.bfloat16), vb, preferred_element_type=jnp.float32)
        n_new = a_old * nvec + a_new * jnp.sum(ke, axis=0, keepdims=True)
        return c_new, n_new, m_new

    init = (jnp.zeros((HEAD_DIM, HEAD_DIM), jnp.float32), jnp.zeros((1, HEAD_DIM), jnp.float32),
            jnp.zeros((1, 1), jnp.float32))
    lax.fori_loop(0, n_chunks, body, init)


def _mlstm(z, zs, conv_qk, gain, batch, seq):
    nh = MLSTM_HEADS
    hb = HEAD_DIM

    def seg(off):
        return pl.BlockSpec((seq, hb), lambda b, h, off=off: (b, off // hb + h))

    return pl.pallas_call(
        functools.partial(_mlstm_kernel, n_chunks=seq // CHUNK),
        grid=(batch, nh),
        in_specs=[seg(OFF_MQ), seg(OFF_MK), seg(OFF_MV), seg(OFF_MO),
                  pl.BlockSpec((seq, SMALL_COLS), lambda b, h: (b, 0)),
                  pl.BlockSpec((CONV_WIDTH, hb), lambda b, h: (0, h)),
                  pl.BlockSpec((CONV_WIDTH, hb), lambda b, h: (0, nh + h)),
                  pl.BlockSpec((1, hb), lambda b, h: (0, h))],
        out_specs=pl.BlockSpec((seq, hb), lambda b, h: (b, h)),
        out_shape=jax.ShapeDtypeStruct((batch * seq, MLSTM_WIDTH), jnp.bfloat16),
        compiler_params=_cparams(("parallel", "parallel")),
        name="mlstm",
    )(z, z, z, z, zs, conv_qk, conv_qk, gain.reshape(1, MLSTM_WIDTH))


def _retention_kernel(q_ref, k_ref, v_ref, g_ref, cos_ref, sin_ref, lg_ref, gain_ref, o_ref, r_ref, decay_ref,
                      *, n_chunks):
    L = CHUNK
    H = RET_HEADS
    D = HEAD_DIM
    bf = jnp.bfloat16
    row = lax.broadcasted_iota(jnp.int32, (L, L), 0)
    col = lax.broadcasted_iota(jnp.int32, (L, L), 1)
    rel = (row - col).astype(jnp.float32)
    idx = lax.broadcasted_iota(jnp.int32, (L, 1), 0).astype(jnp.float32)
    gain = gain_ref[...]
    scale = HEAD_DIM ** -0.5
    lgs = [lg_ref[h][:, 0:1] for h in range(H)]
    for h in range(H):
        decay_ref[h] = jnp.where(rel >= 0, jnp.exp(lgs[h] * jnp.maximum(rel, 0.0)), 0.0)
    r_ref[...] = jnp.zeros(r_ref.shape, jnp.float32)

    def body(c, carry):
        start = pl.multiple_of(c * L, L)
        cosf = cos_ref[pl.ds(start, L), :]
        sinf = sin_ref[pl.ds(start, L), :]
        q_all = q_ref[pl.ds(start, L), :]
        k_all = k_ref[pl.ds(start, L), :]
        v_all = v_ref[pl.ds(start, L), :]
        heads = []
        for h in range(H):
            sl = slice(h * D, (h + 1) * D)

            def rot(x):
                return x * cosf + pltpu.roll(x, D // 2, 1) * sinf

            k = rot(k_all[:, sl]) * scale
            heads.append(dict(sl=sl, qb=rot(q_all[:, sl]).astype(bf), kb=k.astype(bf), vb=v_all[:, sl].astype(bf),
                              kz=k * jnp.exp(lgs[h] * (L - 1.0 - idx))))
        for h, s in enumerate(heads):
            s["qk"] = lax.dot_general(s["qb"], s["kb"], _NT, preferred_element_type=jnp.float32)
            s["qr"] = jnp.dot(s["qb"], r_ref[h].astype(bf), preferred_element_type=jnp.float32)
            s["kv"] = jnp.dot(s["kz"].T.astype(bf), s["vb"], preferred_element_type=jnp.float32)
        for h, s in enumerate(heads):
            s["sv"] = jnp.dot((s["qk"] * decay_ref[h]).astype(bf), s["vb"], preferred_element_type=jnp.float32)
        outs = []
        for h, s in enumerate(heads):
            out = s["sv"] + jnp.exp(lgs[h] * (idx + 1.0)) * s["qr"]
            outs.append(_head_norm(out, gain[:, s["sl"]]))
            r_ref[h] = jnp.exp(lgs[h] * L) * r_ref[h] + s["kv"]
        gate = g_ref[pl.ds(start, L), :]
        o_ref[pl.ds(start, L), :] = (gate * _sigmoid(gate) * jnp.concatenate(outs, axis=1)).astype(o_ref.dtype)
        return carry

    lax.fori_loop(0, n_chunks, body, 0)


def _retention(z, cosf, sinf, lg, gain, batch, seq):
    nh = RET_HEADS
    hb = HEAD_DIM
    width = RET_WIDTH

    def seg(off):
        assert off % width == 0
        return pl.BlockSpec((seq, width), lambda b, off=off: (b, off // width))

    table = pl.BlockSpec((seq, hb), lambda b: (0, 0))
    return pl.pallas_call(
        functools.partial(_retention_kernel, n_chunks=seq // CHUNK),
        grid=(batch,),
        in_specs=[seg(OFF_RQ), seg(OFF_RK), seg(OFF_RV), seg(OFF_RG), table, table,
                  pl.BlockSpec((nh, 1, hb), lambda b: (0, 0, 0)),
                  pl.BlockSpec((1, width), lambda b: (0, 0))],
        out_specs=pl.BlockSpec((seq, width), lambda b: (b, 0)),
        out_shape=jax.ShapeDtypeStruct((batch * seq, width), jnp.bfloat16),
        scratch_shapes=[pltpu.VMEM((nh, hb, hb), jnp.float32),
                        pltpu.VMEM((nh, CHUNK, CHUNK), jnp.float32)],
        compiler_params=_cparams(("parallel",)),
        name="retention",
    )(z, z, z, z, cosf, sinf, lg, gain.reshape(1, width))


def _gelu_tanh(x):
    return 0.5 * x * (1.0 + jnp.tanh(math.sqrt(2.0 / math.pi) * (x + 0.044715 * (x * x * x))))


def _compress_kernel(k_ref, v_ref, pk_ref, w1k_ref, w2k_ref, pv_ref, w1v_ref, w2v_ref, ko_ref, vo_ref, pad_ref,
                     *, seq):
    nblk = seq // CMP_STRIDE
    for x_ref, p_ref, w1_ref, w2_ref, o_ref, transposed in ((k_ref, pk_ref, w1k_ref, w2k_ref, ko_ref, False),
                                                             (v_ref, pv_ref, w1v_ref, w2v_ref, vo_ref, True)):
        pad_ref[0:seq, :] = x_ref[...]
        pad_ref[seq:seq + CMP_BLOCK, :] = jnp.zeros((CMP_BLOCK, HEAD_DIM), jnp.float32)
        acc = jnp.zeros((nblk, HEAD_DIM), jnp.float32)
        for p in range(CMP_BLOCK):
            xs = pad_ref[pl.ds(p, nblk, stride=CMP_STRIDE), :] + p_ref[p:p + 1, :]
            acc = acc + jnp.dot(xs.astype(jnp.bfloat16), w1_ref[p * HEAD_DIM:(p + 1) * HEAD_DIM, :],
                                preferred_element_type=jnp.float32)
        mid = _gelu_tanh(acc).astype(jnp.bfloat16)
        if transposed:
            out = lax.dot_general(w2_ref[...], mid, _NT, preferred_element_type=jnp.float32)
        else:
            out = jnp.dot(mid, w2_ref[...], preferred_element_type=jnp.float32)
        o_ref[0, 0] = out.astype(o_ref.dtype)


def _compress(z, pos_k, w1_k, w2_k, pos_v, w1_v, w2_v, batch, seq):
    hb = HEAD_DIM
    nblk = seq // CMP_STRIDE
    cmp_in = CMP_BLOCK * hb

    def seg(off):
        return pl.BlockSpec((seq, hb), lambda b, g, off=off: (b, off // hb + g))

    def full(shape):
        return pl.BlockSpec(shape, lambda b, g: (0,) * len(shape))

    out_specs = [pl.BlockSpec((1, 1, nblk, hb), lambda b, g: (b, g, 0, 0)),
                 pl.BlockSpec((1, 1, hb, nblk), lambda b, g: (b, g, 0, 0))]
    out_shape = [jax.ShapeDtypeStruct((batch, NSA_KV_GROUPS, nblk, hb), jnp.bfloat16),
                 jax.ShapeDtypeStruct((batch, NSA_KV_GROUPS, hb, nblk), jnp.bfloat16)]
    return pl.pallas_call(
        functools.partial(_compress_kernel, seq=seq),
        grid=(batch, NSA_KV_GROUPS),
        in_specs=[seg(OFF_NKC), seg(OFF_NVC),
                  full((CMP_BLOCK, hb)), full((cmp_in, hb)), full((hb, hb)),
                  full((CMP_BLOCK, hb)), full((cmp_in, hb)), full((hb, hb))],
        out_specs=out_specs,
        out_shape=out_shape,
        scratch_shapes=[pltpu.VMEM((seq + CMP_BLOCK, hb), jnp.float32)],
        compiler_params=_cparams(("parallel", "parallel")),
        name="nsa_compress",
    )(z, z, pos_k, w1_k, w2_k, pos_v, w1_v, w2_v)


def _rel_bucket_np(dist):
    n = np.maximum(dist, 0)
    exact = REL_BUCKETS // 2
    log_ratio = (np.log(np.maximum(n, 1).astype(np.float32) / np.float32(exact))
                 / np.float32(math.log(REL_MAX_DIST / exact)))
    large = np.minimum(exact + (log_ratio * np.float32(REL_BUCKETS - exact)).astype(np.int32), REL_BUCKETS - 1)
    return np.where(n < exact, n, large).astype(np.int32)


def _bias_kernel(tbl_ref, idxc_ref, idxt_ref, bc_ref, bt_ref):
    h = pl.program_id(0)
    idxc = idxc_ref[...]
    idxt = idxt_ref[...]
    accc = jnp.zeros(idxc.shape, jnp.float32)
    acct = jnp.zeros(idxt.shape, jnp.float32)
    for b in range(REL_BUCKETS):
        val = tbl_ref[b, h]
        accc = jnp.where(idxc == b, val, accc)
        acct = jnp.where(idxt == b, val, acct)
    bc_ref[0] = accc
    bt_ref[:, 0] = acct


def _bias_tables(rel_bias, seq):
    nblk = seq // CMP_STRIDE
    nq = seq // TILE
    j = np.arange(nblk)[:, None]
    t = np.arange(seq)[None, :]
    idx_c = _rel_bucket_np(t - (j * CMP_STRIDE + CMP_BLOCK - 1)).reshape(nblk, nq, TILE).transpose(1, 0, 2)
    kk = np.arange(TILE)[:, None]
    i = np.arange(TILE)[None, :]
    idx_t = np.stack([_rel_bucket_np(i - kk + d * TILE) for d in range(3)])
    return pl.pallas_call(
        _bias_kernel,
        grid=(NSA_HEADS,),
        in_specs=[pl.BlockSpec(memory_space=pltpu.SMEM),
                  pl.BlockSpec((nq, nblk, TILE), lambda h: (0, 0, 0)),
                  pl.BlockSpec((3, TILE, TILE), lambda h: (0, 0, 0))],
        out_specs=[pl.BlockSpec((1, nq, nblk, TILE), lambda h: (h, 0, 0, 0)),
                   pl.BlockSpec((3, 1, TILE, TILE), lambda h: (0, h, 0, 0))],
        out_shape=[jax.ShapeDtypeStruct((NSA_HEADS, nq, nblk, TILE), jnp.float32),
                   jax.ShapeDtypeStruct((3, NSA_HEADS, TILE, TILE), jnp.float32)],
        compiler_params=_cparams(("parallel",)),
        name="nsa_bias_tables",
    )(rel_bias, jnp.asarray(idx_c), jnp.asarray(idx_t))


def _nsa_kernel(q_ref, kc_ref, vct_ref, ks_ref, vs_ref, kw_ref, vw_ref, sm_ref, bc_ref, bt_ref, o_ref,
                kb_ref, vt_ref, *scratch, seq):
    T = TILE
    bf = jnp.bfloat16
    for slot, (k_ref, v_ref) in enumerate(((ks_ref, vs_ref), (kw_ref, vw_ref))):
        for c in range(seq // T):
            kb_ref[slot, c] = k_ref[c * T:(c + 1) * T, :].astype(bf)
            vt_ref[slot, c] = v_ref[c * T:(c + 1) * T, :].T.astype(bf)

    def step(n, carry):
        rows = pl.ds(pl.multiple_of(n * T, T), T)
        _nsa_tile(n, q_ref.at[rows, :], kc_ref, vct_ref, sm_ref.at[rows, :], bc_ref.at[:, n], bt_ref,
                  o_ref.at[rows, :], kb_ref, vt_ref, *scratch, seq=seq)
        return carry

    lax.fori_loop(0, seq // T, step, 0)


def _nsa_tile(n, q_ref, kc_ref, vct_ref, sm_ref, bc_ref, bt_ref, o_ref,
              kb_ref, vt_ref, acc_ref, sw_ref, ss_ref, sel_ref, gate_ref, *, seq):
    R = NSA_GROUP_SIZE
    T = TILE
    nblk = seq // CMP_STRIDE
    n_sel = seq // SEL_BLOCK
    top = min(SEL_TOPK, n_sel)
    bf = jnp.bfloat16
    g = pl.program_id(1)
    t0 = n * T

    q = q_ref[...] * (HEAD_DIM ** -0.5)
    qt = jnp.concatenate([q[:, r * HEAD_DIM:(r + 1) * HEAD_DIM].T for r in range(R)], axis=1).astype(bf)

    def rep(x):
        return jnp.concatenate([x] * R, axis=1)

    j_c = lax.broadcasted_iota(jnp.int32, (nblk, T), 0)
    t_c = t0 + lax.broadcasted_iota(jnp.int32, (nblk, T), 1)
    valid_c = rep(t_c >= j_c * CMP_STRIDE + (CMP_BLOCK - 1))
    s_c = (jnp.dot(kc_ref[0, 0], qt, preferred_element_type=jnp.float32)
           + jnp.concatenate([bc_ref[r] for r in range(R)], axis=1))
    s_c = jnp.where(valid_c, s_c, NEG_INF)
    e_c = jnp.exp(s_c - jnp.max(s_c, axis=0, keepdims=True))
    p_c = jnp.where(valid_c, e_c * (1.0 / jnp.sum(e_c, axis=0, keepdims=True)), 0.0)
    o_c = jnp.dot(vct_ref[0, 0], p_c.astype(bf), preferred_element_type=jnp.float32)

    p_sum = p_c[:, 0:T]
    for r in range(1, R):
        p_sum = p_sum + p_c[:, r * T:(r + 1) * T]
    assert n_sel % 8 == 0 and n_sel <= T
    ov_s = lax.broadcasted_iota(jnp.int32, (n_sel, nblk), 0)
    ov_j = lax.broadcasted_iota(jnp.int32, (n_sel, nblk), 1)
    ratio = SEL_BLOCK // CMP_STRIDE
    span = CMP_BLOCK // CMP_STRIDE
    overlap = ((ov_j < ratio * ov_s + ratio) & (ov_j + span > ratio * ov_s)).astype(jnp.float32)
    imp = jnp.dot(overlap, p_sum, preferred_element_type=jnp.float32, precision=lax.Precision.HIGHEST)

    blk = lax.broadcasted_iota(jnp.int32, (n_sel, T), 0)
    cur = (t0 + lax.broadcasted_iota(jnp.int32, (n_sel, T), 1)) // SEL_BLOCK
    forced = (blk == 0) | (blk == cur) | (blk == cur - 1)
    val = jnp.where(forced, jnp.inf, jnp.where(blk > cur, -jnp.inf, imp))
    blk_f = blk.astype(jnp.float32)
    sel = jnp.zeros((n_sel, T), jnp.float32)
    for _ in range(top):
        best = jnp.max(val, axis=0, keepdims=True)
        first = jnp.min(jnp.where(val == best, blk_f, float(n_sel)), axis=0, keepdims=True)
        hit = blk_f == first
        sel = jnp.where(hit, 1.0, sel)
        val = jnp.where(hit, -jnp.inf, val)
    sel_ref[...] = sel

    rows = lax.broadcasted_iota(jnp.int32, (T, T), 0)
    t_q = t0 + lax.broadcasted_iota(jnp.int32, (T, T), 1)

    def scores(kv, c, allowed):
        mask_add = jnp.where(allowed, 0.0, NEG_INF)
        c_read = jnp.clip(c, 0, n)
        tile_dist = jnp.minimum(n - c_read, 2)
        bias = jnp.concatenate([bt_ref[tile_dist, r] + mask_add for r in range(R)], axis=1)
        return jnp.dot(kb_ref[kv, c_read], qt, preferred_element_type=jnp.float32) + bias

    def col_max(tiles):
        out = jnp.max(tiles[0], axis=0, keepdims=True)
        for s in tiles[1:]:
            out = jnp.maximum(out, jnp.max(s, axis=0, keepdims=True))
        return out

    def weighted_values(kv, tile_ids, probs):
        vt = jnp.concatenate([vt_ref[kv, jnp.clip(c, 0, n)] for c in tile_ids], axis=1)
        total = probs[0].sum(axis=0, keepdims=True)
        for p in probs[1:]:
            total = total + p.sum(axis=0, keepdims=True)
        pv = jnp.dot(vt, jnp.concatenate(probs, axis=0).astype(bf), preferred_element_type=jnp.float32)
        return pv, total

    n_win = WINDOW // T + 1
    win_ids = [n - (n_win - 1) + w for w in range(n_win)]
    for w, c in enumerate(win_ids):
        kpos = c * T + rows
        dist = t_q - kpos
        sw_ref[w] = scores(1, c, (dist >= 0) & (dist < WINDOW) & (kpos >= 0))
    m_w = col_max([sw_ref[w] for w in range(n_win)])
    acc_w, l_w = weighted_values(1, win_ids, [jnp.exp(sw_ref[w] - m_w) for w in range(n_win)])
    o_w = acc_w * (1.0 / l_w)

    blocks_per_tile = T // SEL_BLOCK
    trips = (n + SEL_GROUP) // SEL_GROUP

    def sel_scores(c):
        kpos = c * T + rows
        chosen = sel_ref[pl.ds(c * blocks_per_tile, 1), :]
        for i in range(1, blocks_per_tile):
            chosen = jnp.where(rows < i * SEL_BLOCK, chosen, sel_ref[pl.ds(c * blocks_per_tile + i, 1), :])
        return scores(0, c, (chosen > 0.5) & (kpos <= t_q))

    def sel_pass1(i, m_run):
        ids = [SEL_GROUP * i + u for u in range(SEL_GROUP)]
        tiles = [sel_scores(c) for c in ids]
        for c, s in zip(ids, tiles):
            ss_ref[c] = s
        return jnp.maximum(m_run, col_max(tiles))

    m_s = lax.fori_loop(0, trips, sel_pass1, jnp.full((1, R * T), NEG_INF, jnp.float32))
    acc_ref[...] = jnp.zeros(acc_ref.shape, jnp.float32)

    def sel_pass2(i, l_run):
        ids = [SEL_GROUP * i + u for u in range(SEL_GROUP)]
        pv, total = weighted_values(0, ids, [jnp.exp(ss_ref[c] - m_s) for c in ids])
        acc_ref[...] += pv
        return l_run + total

    l_s = lax.fori_loop(0, trips, sel_pass2, jnp.zeros((1, R * T), jnp.float32))
    o_s = acc_ref[...] * (1.0 / l_s)

    gate_ref[...] = _sigmoid(sm_ref[...]).T
    outs = []
    for r in range(R):
        def gate(branch):
            return gate_ref[pl.ds(SM_GATE + branch * NSA_HEADS + g * R + r, 1), :]
        sl = slice(r * T, (r + 1) * T)
        outs.append((gate(0) * o_c[:, sl] + gate(1) * o_s[:, sl] + gate(2) * o_w[:, sl]).T)
    o_ref[...] = jnp.concatenate(outs, axis=1).astype(o_ref.dtype)


def _nsa(z, zs, k_c, v_c, bias_c, bias_t, batch, seq):
    hb = HEAD_DIM
    R = NSA_GROUP_SIZE
    nblk = seq // CMP_STRIDE
    nq = seq // TILE
    gw = R * hb
    assert OFF_NQ % gw == 0

    def seg(off):
        return pl.BlockSpec((seq, hb), lambda b, g, off=off: (b, off // hb + g))

    return pl.pallas_call(
        functools.partial(_nsa_kernel, seq=seq),
        grid=(batch, NSA_KV_GROUPS),
        in_specs=[pl.BlockSpec((seq, gw), lambda b, g: (b, OFF_NQ // gw + g)),
                  pl.BlockSpec((1, 1, nblk, hb), lambda b, g: (b, g, 0, 0)),
                  pl.BlockSpec((1, 1, hb, nblk), lambda b, g: (b, g, 0, 0)),
                  seg(OFF_NKS), seg(OFF_NVS), seg(OFF_NKW), seg(OFF_NVW),
                  pl.BlockSpec((seq, SMALL_COLS), lambda b, g: (b, 0)),
                  pl.BlockSpec((R, nq, nblk, TILE), lambda b, g: (g, 0, 0, 0)),
                  pl.BlockSpec((3, R, TILE, TILE), lambda b, g: (0, g, 0, 0))],
        out_specs=pl.BlockSpec((seq, gw), lambda b, g: (b, g)),
        out_shape=jax.ShapeDtypeStruct((batch * seq, NSA_WIDTH), jnp.bfloat16),
        scratch_shapes=[pltpu.VMEM((2, nq, TILE, hb), jnp.bfloat16),
                        pltpu.VMEM((2, nq, hb, TILE), jnp.bfloat16),
                        pltpu.VMEM((hb, R * TILE), jnp.float32),
                        pltpu.VMEM((WINDOW // TILE + 1, TILE, R * TILE), jnp.float32),
                        pltpu.VMEM((nq, TILE, R * TILE), jnp.float32),
                        pltpu.VMEM((seq // SEL_BLOCK, TILE), jnp.float32),
                        pltpu.VMEM((SMALL_COLS, TILE), jnp.float32)],
        compiler_params=_cparams(("parallel", "parallel")),
        name="nsa_attention",
    )(z, k_c, v_c, z, z, z, z, zs, bias_c, bias_t)


def _rope_tables(seq):
    half = HEAD_DIM // 2
    inv_freq = 1.0 / (10000.0 ** jnp.linspace(0.0, 1.0, half))
    ang = jnp.arange(seq).astype(jnp.float32)[:, None] * inv_freq[None, :]
    cos, sin = jnp.cos(ang), jnp.sin(ang)
    return jnp.concatenate([cos, cos], axis=-1), jnp.concatenate([-sin, sin], axis=-1)


def _retention_log_decay():
    lg = np.log1p(-np.exp2(-5.0 - np.arange(RET_HEADS, dtype=np.float64))).astype(np.float32)
    return jnp.asarray(np.broadcast_to(lg[:, None, None], (RET_HEADS, 1, HEAD_DIM)).copy())


def _split_in_proj(w_in, b_in, w_gate_down):
    bf = jnp.bfloat16
    a0 = 4 * MLSTM_WIDTH
    a1 = a0 + 2 * MLSTM_HEADS
    a2 = a1 + NSA_WIDTH + 6 * NSA_KV_WIDTH
    a3 = a2 + 3 * NSA_HEADS
    depth, d, _ = w_in.shape
    w_main = jnp.concatenate([w_gate_down.astype(bf), w_in[:, :, :a0].astype(bf), w_in[:, :, a1:a2].astype(bf),
                              w_in[:, :, a3:].astype(bf)], axis=2)
    b_main = jnp.concatenate([jnp.zeros((depth, GATE_RANK), jnp.float32), b_in[:, :a0], b_in[:, a1:a2],
                              b_in[:, a3:]], axis=1)
    n_small = (a1 - a0) + (a3 - a2)
    w_small = jnp.concatenate([w_in[:, :, a0:a1].astype(bf), w_in[:, :, a2:a3].astype(bf),
                               jnp.zeros((depth, d, SMALL_COLS - n_small), bf)], axis=2)
    b_small = jnp.concatenate([b_in[:, a0:a1], b_in[:, a2:a3],
                               jnp.zeros((depth, SMALL_COLS - n_small), jnp.float32)], axis=1)
    return w_main, b_main, w_small, b_small


def kernel(x, rel_bias, norm_mix_g, w_in, b_in, conv_qk, mlstm_norm_g, cmp_pos_k, cmp_w1_k, cmp_w2_k, cmp_pos_v, cmp_w1_v, cmp_w2_v, ret_norm_g, w_br_mlstm, w_br_nsa, w_br_ret, w_gate_down, w_gate_up, b_gate, w_out, norm_mlp_g, w_up, w_down, final_norm_g):
    batch, seq, d = x.shape
    depth = w_in.shape[0]
    bf = jnp.bfloat16
    f32 = jnp.float32
    bias_c, bias_t = _bias_tables(rel_bias, seq)
    cosf, sinf = _rope_tables(seq)
    lg = _retention_log_decay()
    w_main, b_main, w_small, b_small = _split_in_proj(w_in, b_in, w_gate_down)
    w_up_bf = w_up.astype(bf)
    w_down_bf = w_down.astype(bf)
    w1_k, w2_k, w1_v, w2_vt = cmp_w1_k.astype(bf), cmp_w2_k.astype(bf), cmp_w1_v.astype(bf), jnp.swapaxes(cmp_w2_v, 1, 2).astype(bf)
    xr = x.reshape(batch * seq, d)
    for l in range(depth):
        h = _rmsnorm(xr, norm_mix_g[l], bf)
        z = _matmul(h, w_main, l, bias=b_main, out_dtype=f32, tm=1024, tn=1280, tk=D_MODEL, name="in_proj")
        zs = _matmul(h, w_small, l, bias=b_small, out_dtype=f32, tm=1024, tn=SMALL_COLS, tk=D_MODEL,
                     name="in_proj_small")
        y_a = _mlstm(z, zs, conv_qk[l], mlstm_norm_g[l], batch, seq)
        k_c, v_c = _compress(z, cmp_pos_k[l], w1_k[l], w2_k[l], cmp_pos_v[l], w1_v[l], w2_vt[l], batch, seq)
        y_b = _nsa(z, zs, k_c, v_c, bias_c, bias_t, batch, seq)
        y_c = _retention(z, cosf, sinf, lg, ret_norm_g[l], batch, seq)
        merged = _merge(z, y_a, y_b, y_c, l, w_gate_up, b_gate, w_br_mlstm, w_br_nsa, w_br_ret)
        xr = _matmul(merged, w_out, l, res=xr, out_dtype=f32, tm=1024, tn=512, tk=D_MODEL, name="out_proj")
        h = _rmsnorm(xr, norm_mlp_g[l], bf)
        up = _matmul(h, w_up_bf, l, act="relu2", out_dtype=bf, tm=1024, tn=1024, tk=D_MODEL, name="mlp_up")
        xr = _matmul(up, w_down_bf, l, res=xr, out_dtype=f32, tm=1024, tn=512, tk=4096, name="mlp_down")
    return _rmsnorm(xr, final_norm_g, f32).reshape(batch, seq, d)
```

```python
import functools
import math

import numpy as np
import jax
import jax.numpy as jnp
from jax import lax
from jax.experimental import pallas as pl
from jax.experimental.pallas import tpu as pltpu

D_MODEL = 4096
HEAD_DIM = 128
MIX_WIDTH = D_MODEL // 2
MLSTM_HEADS = MIX_WIDTH // (4 * HEAD_DIM)
NSA_HEADS = MIX_WIDTH // (2 * HEAD_DIM)
RET_HEADS = MIX_WIDTH // (4 * HEAD_DIM)
NSA_KV_GROUPS = 2
NSA_GROUP_SIZE = NSA_HEADS // NSA_KV_GROUPS
MLSTM_WIDTH = MLSTM_HEADS * HEAD_DIM
NSA_WIDTH = NSA_HEADS * HEAD_DIM
RET_WIDTH = RET_HEADS * HEAD_DIM
NSA_KV_WIDTH = NSA_KV_GROUPS * HEAD_DIM
CONV_WIDTH = 4
CMP_BLOCK = 32
CMP_STRIDE = 16
SEL_BLOCK = 64
SEL_TOPK = 8
WINDOW = 512
REL_BUCKETS = 32
REL_MAX_DIST = 128
GATE_RANK = D_MODEL // 4
N_BRANCH = 3
EPS = 1e-6
NEG_INF = -1e30

CHUNK = 128
TILE = 128
SEL_GROUP = 4

OFF_GL = 0
OFF_MQ = OFF_GL + GATE_RANK
OFF_MK = OFF_MQ + MLSTM_WIDTH
OFF_MV = OFF_MK + MLSTM_WIDTH
OFF_MO = OFF_MV + MLSTM_WIDTH
OFF_NQ = OFF_MO + MLSTM_WIDTH
OFF_NKC = OFF_NQ + NSA_WIDTH
OFF_NVC = OFF_NKC + NSA_KV_WIDTH
OFF_NKS = OFF_NVC + NSA_KV_WIDTH
OFF_NVS = OFF_NKS + NSA_KV_WIDTH
OFF_NKW = OFF_NVS + NSA_KV_WIDTH
OFF_NVW = OFF_NKW + NSA_KV_WIDTH
OFF_RQ = OFF_NVW + NSA_KV_WIDTH
OFF_RK = OFF_RQ + RET_WIDTH
OFF_RV = OFF_RK + RET_WIDTH
OFF_RG = OFF_RV + RET_WIDTH
MAIN_COLS = OFF_RG + RET_WIDTH
SMALL_COLS = 128
SM_I = 0
SM_F = MLSTM_HEADS
SM_GATE = 2 * MLSTM_HEADS

VMEM_LIMIT = 56 * 1024 * 1024

_NT = (((1,), (1,)), ((), ()))


def _sigmoid(x):
    return 1.0 / (1.0 + jnp.exp(-x))


def _cparams(sem):
    return pltpu.CompilerParams(dimension_semantics=sem, vmem_limit_bytes=VMEM_LIMIT)


def _rmsnorm_kernel(x_ref, g_ref, o_ref):
    x = x_ref[...]
    ms = jnp.mean(x * x, axis=-1, keepdims=True)
    o_ref[...] = (x * lax.rsqrt(ms + EPS) * g_ref[...]).astype(o_ref.dtype)


def _rmsnorm(x, gain, out_dtype, tm=256):
    t, d = x.shape
    return pl.pallas_call(
        _rmsnorm_kernel,
        grid=(t // tm,),
        in_specs=[pl.BlockSpec((tm, d), lambda i: (i, 0)),
                  pl.BlockSpec((1, d), lambda i: (0, 0))],
        out_specs=pl.BlockSpec((tm, d), lambda i: (i, 0)),
        out_shape=jax.ShapeDtypeStruct((t, d), out_dtype),
        compiler_params=_cparams(("parallel",)),
        name="rmsnorm",
    )(x, gain.reshape(1, d))


def _mm_kernel(*refs, nk, act, has_bias, has_res, cast_w, acc_in_out):
    a_ref, w_ref = refs[0], refs[1]
    pos = 2
    b_ref = refs[pos] if has_bias else None
    pos += int(has_bias)
    r_ref = refs[pos] if has_res else None
    pos += int(has_res)
    o_ref = refs[pos]
    pos += 1
    has_acc = nk > 1 and not acc_in_out
    acc_ref = refs[pos] if has_acc else None
    pos += int(has_acc)

    if cast_w:
        wb_ref = refs[pos]

        @pl.when(pl.program_id(1) == 0)
        def _():
            wb_ref[...] = w_ref[...].astype(jnp.bfloat16)

        w_ref = wb_ref

    if acc_in_out:
        k = pl.program_id(2)

        @pl.when(k == 0)
        def _():
            o_ref[...] = r_ref[...] if has_res else jnp.zeros(o_ref.shape, o_ref.dtype)

        o_ref[...] += jnp.dot(a_ref[...], w_ref[...], preferred_element_type=jnp.float32)
        return

    part = jnp.dot(a_ref[...], w_ref[...], preferred_element_type=jnp.float32)

    def finish(acc):
        if has_bias:
            acc = acc + b_ref[...]
        if act == "relu2":
            acc = jnp.square(jnp.maximum(acc, 0.0))
        if has_res:
            acc = acc + r_ref[...]
        o_ref[...] = acc.astype(o_ref.dtype)

    if nk == 1:
        finish(part)
    else:
        k = pl.program_id(2)

        @pl.when(k == 0)
        def _():
            acc_ref[...] = part

        @pl.when(k > 0)
        def _():
            acc_ref[...] += part

        @pl.when(k == nk - 1)
        def _():
            finish(acc_ref[...])


def _matmul(a, w, layer, *, bias=None, res=None, act=None, out_dtype, tm, tn, tk, name):
    m, kdim = a.shape
    n = w.shape[2]
    tm, tn, tk = min(tm, m), min(tn, n), min(tk, kdim)
    nk = kdim // tk
    cast_w = w.dtype != jnp.bfloat16
    assert not (cast_w and nk > 1)
    in_specs = [pl.BlockSpec((tm, tk), lambda j, i, k: (i, k)),
                pl.BlockSpec((None, tk, tn), lambda j, i, k: (layer, k, j))]
    args = [a, w]
    if bias is not None:
        in_specs.append(pl.BlockSpec((None, 1, tn), lambda j, i, k: (layer, 0, j)))
        args.append(bias.reshape(bias.shape[0], 1, n))
    if res is not None:
        in_specs.append(pl.BlockSpec((tm, tn), lambda j, i, k: (i, j)))
        args.append(res)
    acc_in_out = nk > 1 and out_dtype == jnp.float32 and act is None and bias is None
    scratch = [pltpu.VMEM((tm, tn), jnp.float32)] if nk > 1 and not acc_in_out else []
    if cast_w:
        scratch.append(pltpu.VMEM((tk, tn), jnp.bfloat16))
    return pl.pallas_call(
        functools.partial(_mm_kernel, nk=nk, act=act, has_bias=bias is not None, has_res=res is not None,
                          cast_w=cast_w, acc_in_out=acc_in_out),
        grid=(n // tn, m // tm, nk),
        in_specs=in_specs,
        out_specs=pl.BlockSpec((tm, tn), lambda j, i, k: (i, j)),
        out_shape=jax.ShapeDtypeStruct((m, n), out_dtype),
        scratch_shapes=scratch,
        compiler_params=_cparams(("parallel", "arbitrary", "arbitrary")),
        name=name,
    )(*args)


def _merge_kernel(gl_ref, ya_ref, yb_ref, yc_ref, wg0_ref, wg1_ref, wg2_ref, bg0_ref, bg1_ref, bg2_ref,
                  wa_ref, wb_ref, wc_ref, o_ref, *wbf_refs):
    @pl.when(pl.program_id(1) == 0)
    def _():
        for src, dst in zip((wg0_ref, wg1_ref, wg2_ref, wa_ref, wb_ref, wc_ref), wbf_refs):
            dst[...] = src[...].astype(jnp.bfloat16)

    wg0, wg1, wg2, wa, wb, wc = wbf_refs
    gl = gl_ref[...].astype(jnp.bfloat16)

    def branch(wg, bg_ref, y_ref, w):
        gate = _sigmoid(jnp.dot(gl, wg[...], preferred_element_type=jnp.float32) + bg_ref[...])
        return gate * jnp.dot(y_ref[...], w[...], preferred_element_type=jnp.float32)

    out = branch(wg0, bg0_ref, ya_ref, wa) + branch(wg1, bg1_ref, yb_ref, wb) + branch(wg2, bg2_ref, yc_ref, wc)
    o_ref[...] = out.astype(o_ref.dtype)


def _merge(z, ya, yb, yc, layer, w_gate_up, b_gate, wa, wb, wc, tm=1024, tn=512):
    t = z.shape[0]
    d = wa.shape[2]
    tm = min(tm, t)
    nj = d // tn
    assert OFF_GL % GATE_RANK == 0
    gl_blk = OFF_GL // GATE_RANK

    def wg_spec(i):
        return pl.BlockSpec((None, GATE_RANK, tn), lambda j, m, i=i: (layer, 0, i * nj + j))

    def bg_spec(i):
        return pl.BlockSpec((None, 1, tn), lambda j, m, i=i: (layer, 0, i * nj + j))

    def row_spec(width):
        return pl.BlockSpec((tm, width), lambda j, m: (m, 0))

    def w_spec(width):
        return pl.BlockSpec((None, width, tn), lambda j, m: (layer, 0, j))

    bg = b_gate.reshape(b_gate.shape[0], 1, N_BRANCH * d)
    bf_scratch = [pltpu.VMEM((rows, tn), jnp.bfloat16)
                  for rows in (GATE_RANK, GATE_RANK, GATE_RANK, MLSTM_WIDTH, NSA_WIDTH, RET_WIDTH)]
    return pl.pallas_call(
        _merge_kernel,
        grid=(nj, t // tm),
        in_specs=[pl.BlockSpec((tm, GATE_RANK), lambda j, m: (m, gl_blk)),
                  row_spec(MLSTM_WIDTH), row_spec(NSA_WIDTH), row_spec(RET_WIDTH),
                  wg_spec(0), wg_spec(1), wg_spec(2), bg_spec(0), bg_spec(1), bg_spec(2),
                  w_spec(MLSTM_WIDTH), w_spec(NSA_WIDTH), w_spec(RET_WIDTH)],
        out_specs=pl.BlockSpec((tm, tn), lambda j, m: (m, j)),
        out_shape=jax.ShapeDtypeStruct((t, d), jnp.bfloat16),
        scratch_shapes=bf_scratch,
        compiler_params=_cparams(("parallel", "arbitrary")),
        name="gated_merge",
    )(z, ya, yb, yc, w_gate_up, w_gate_up, w_gate_up, bg, bg, bg, wa, wb, wc)


def _head_norm(h, gain):
    mu = jnp.mean(h, axis=-1, keepdims=True)
    d = h - mu
    var = jnp.mean(d * d, axis=-1, keepdims=True)
    return d * lax.rsqrt(var + EPS) * gain


def _mlstm_kernel(q_ref, k_ref, v_ref, og_ref, sm_ref, cq_ref, ck_ref, gain_ref, o_ref, *, n_chunks):
    L = CHUNK
    h = pl.program_id(1)
    lane_ids = lax.broadcasted_iota(jnp.int32, (L, SMALL_COLS), 1)
    row = lax.broadcasted_iota(jnp.int32, (L, L), 0)
    col = lax.broadcasted_iota(jnp.int32, (L, L), 1)
    causal = col <= row
    eye = col == row
    cq = cq_ref[...]
    ck = ck_ref[...]
    gain = gain_ref[...]
    scale = HEAD_DIM ** -0.5

    def conv_silu(ref, w, c, start):
        cur = ref[pl.ds(start, L), :]
        pstart = pl.multiple_of(jnp.maximum(start - 8, 0), 8)
        prev = jnp.where(c > 0, ref[pl.ds(pstart, 8), :], 0.0)
        ext = jnp.concatenate([prev, cur], axis=0)
        y = w[CONV_WIDTH - 1:CONV_WIDTH, :] * cur
        for kk in range(CONV_WIDTH - 1):
            off = 8 - (CONV_WIDTH - 1) + kk
            y = y + w[kk:kk + 1, :] * ext[off:off + L, :]
        return y * _sigmoid(y)

    def lane_pick(x, lane):
        return jnp.sum(jnp.where(lane_ids == lane, x, 0.0), axis=1, keepdims=True)

    def body(c, carry):
        cmat, nvec, m = carry
        start = pl.multiple_of(c * L, L)
        q = conv_silu(q_ref, cq, c, start)
        k = conv_silu(k_ref, ck, c, start) * scale
        v = v_ref[pl.ds(start, L), :]
        sm = sm_ref[pl.ds(start, L), :]
        i_col = lane_pick(sm, SM_I + h)
        f_col = lane_pick(sm, SM_F + h)
        ls_col = jnp.minimum(f_col, 0.0) - jnp.log(1.0 + jnp.exp(-jnp.abs(f_col)))
        a_row = jnp.sum(jnp.where(row <= col, ls_col, 0.0), axis=0, keepdims=True)
        a_col = jnp.sum(jnp.where(eye, a_row, 0.0), axis=1, keepdims=True)
        i_row = jnp.sum(jnp.where(eye, i_col, 0.0), axis=0, keepdims=True)
        g = jnp.sum(ls_col, axis=0, keepdims=True)
        w_col = g - a_col + i_col
        m_loc = jnp.max(w_col, axis=0, keepdims=True)
        e_col = jnp.exp(w_col - m_loc)

        log_d = jnp.where(causal, a_col - a_row + i_row, NEG_INF)
        log_inter = a_col + m
        m_row = jnp.maximum(log_inter, jnp.max(log_d, axis=1, keepdims=True))
        inter = jnp.exp(log_inter - m_row)
        dmat = jnp.exp(log_d - m_row)

        qb = q.astype(jnp.bfloat16)
        kb = k.astype(jnp.bfloat16)
        vb = v.astype(jnp.bfloat16)
        sc = lax.dot_general(qb, kb, _NT, preferred_element_type=jnp.float32) * dmat
        num = (inter * jnp.dot(qb, cmat.astype(jnp.bfloat16), preferred_element_type=jnp.float32)
               + jnp.dot(sc.astype(jnp.bfloat16), vb, preferred_element_type=jnp.float32))
        den = inter * jnp.sum(q * nvec, axis=1, keepdims=True) + jnp.sum(sc, axis=1, keepdims=True)
        hout = num / jnp.maximum(jnp.abs(den), jnp.exp(-m_row))

        y = _head_norm(hout, gain)
        o_ref[pl.ds(start, L), :] = (_sigmoid(og_ref[pl.ds(start, L), :]) * y).astype(o_ref.dtype)

        m_new = jnp.maximum(g + m, m_loc)
        a_old = jnp.exp(g + m - m_new)
        a_new = jnp.exp(m_loc - m_new)
        ke = k * e_col
        c_new = a_old * cmat + a_new * jnp.dot(ke.T.astype(jnp.bfloat16), vb, preferred_element_type=jnp.float32)
        n_new = a_old * nvec + a_new * jnp.sum(ke, axis=0, keepdims=True)
        return c_new, n_new, m_new

    init = (jnp.zeros((HEAD_DIM, HEAD_DIM), jnp.float32), jnp.zeros((1, HEAD_DIM), jnp.float32),
            jnp.zeros((1, 1), jnp.float32))
    lax.fori_loop(0, n_chunks, body, init)


def _mlstm(z, zs, conv_qk, gain, batch, seq):
    nh = MLSTM_HEADS
    hb = HEAD_DIM

    def seg(off):
        return pl.BlockSpec((seq, hb), lambda b, h, off=off: (b, off // hb + h))

    return pl.pallas_call(
        functools.partial(_mlstm_kernel, n_chunks=seq // CHUNK),
        grid=(batch, nh),
        in_specs=[seg(OFF_MQ), seg(OFF_MK), seg(OFF_MV), seg(OFF_MO),
                  pl.BlockSpec((seq, SMALL_COLS), lambda b, h: (b, 0)),
                  pl.BlockSpec((CONV_WIDTH, hb), lambda b, h: (0, h)),
                  pl.BlockSpec((CONV_WIDTH, hb), lambda b, h: (0, nh + h)),
                  pl.BlockSpec((1, hb), lambda b, h: (0, h))],
        out_specs=pl.BlockSpec((seq, hb), lambda b, h: (b, h)),
        out_shape=jax.ShapeDtypeStruct((batch * seq, MLSTM_WIDTH), jnp.bfloat16),
        compiler_params=_cparams(("parallel", "parallel")),
        name="mlstm",
    )(z, z, z, z, zs, conv_qk, conv_qk, gain.reshape(1, MLSTM_WIDTH))


def _retention_kernel(q_ref, k_ref, v_ref, g_ref, cos_ref, sin_ref, lg_ref, gain_ref, o_ref, r_ref, decay_ref,
                      *, n_chunks):
    L = CHUNK
    H = RET_HEADS
    D = HEAD_DIM
    bf = jnp.bfloat16
    row = lax.broadcasted_iota(jnp.int32, (L, L), 0)
    col = lax.broadcasted_iota(jnp.int32, (L, L), 1)
    rel = (row - col).astype(jnp.float32)
    idx = lax.broadcasted_iota(jnp.int32, (L, 1), 0).astype(jnp.float32)
    gain = gain_ref[...]
    scale = HEAD_DIM ** -0.5
    lgs = [lg_ref[h][:, 0:1] for h in range(H)]
    for h in range(H):
        decay_ref[h] = jnp.where(rel >= 0, jnp.exp(lgs[h] * jnp.maximum(rel, 0.0)), 0.0)
    r_ref[...] = jnp.zeros(r_ref.shape, jnp.float32)

    def body(c, carry):
        start = pl.multiple_of(c * L, L)
        cosf = cos_ref[pl.ds(start, L), :]
        sinf = sin_ref[pl.ds(start, L), :]
        q_all = q_ref[pl.ds(start, L), :]
        k_all = k_ref[pl.ds(start, L), :]
        v_all = v_ref[pl.ds(start, L), :]
        heads = []
        for h in range(H):
            sl = slice(h * D, (h + 1) * D)

            def rot(x):
                return x * cosf + pltpu.roll(x, D // 2, 1) * sinf

            k = rot(k_all[:, sl]) * scale
            heads.append(dict(sl=sl, qb=rot(q_all[:, sl]).astype(bf), kb=k.astype(bf), vb=v_all[:, sl].astype(bf),
                              kz=k * jnp.exp(lgs[h] * (L - 1.0 - idx))))
        for h, s in enumerate(heads):
            s["qk"] = lax.dot_general(s["qb"], s["kb"], _NT, preferred_element_type=jnp.float32)
            s["qr"] = jnp.dot(s["qb"], r_ref[h].astype(bf), preferred_element_type=jnp.float32)
            s["kv"] = jnp.dot(s["kz"].T.astype(bf), s["vb"], preferred_element_type=jnp.float32)
        for h, s in enumerate(heads):
            s["sv"] = jnp.dot((s["qk"] * decay_ref[h]).astype(bf), s["vb"], preferred_element_type=jnp.float32)
        outs = []
        for h, s in enumerate(heads):
            out = s["sv"] + jnp.exp(lgs[h] * (idx + 1.0)) * s["qr"]
            outs.append(_head_norm(out, gain[:, s["sl"]]))
            r_ref[h] = jnp.exp(lgs[h] * L) * r_ref[h] + s["kv"]
        gate = g_ref[pl.ds(start, L), :]
        o_ref[pl.ds(start, L), :] = (gate * _sigmoid(gate) * jnp.concatenate(outs, axis=1)).astype(o_ref.dtype)
        return carry

    lax.fori_loop(0, n_chunks, body, 0)


def _retention(z, cosf, sinf, lg, gain, batch, seq):
    nh = RET_HEADS
    hb = HEAD_DIM
    width = RET_WIDTH

    def seg(off):
        assert off % width == 0
        return pl.BlockSpec((seq, width), lambda b, off=off: (b, off // width))

    table = pl.BlockSpec((seq, hb), lambda b: (0, 0))
    return pl.pallas_call(
        functools.partial(_retention_kernel, n_chunks=seq // CHUNK),
        grid=(batch,),
        in_specs=[seg(OFF_RQ), seg(OFF_RK), seg(OFF_RV), seg(OFF_RG), table, table,
                  pl.BlockSpec((nh, 1, hb), lambda b: (0, 0, 0)),
                  pl.BlockSpec((1, width), lambda b: (0, 0))],
        out_specs=pl.BlockSpec((seq, width), lambda b: (b, 0)),
        out_shape=jax.ShapeDtypeStruct((batch * seq, width), jnp.bfloat16),
        scratch_shapes=[pltpu.VMEM((nh, hb, hb), jnp.float32),
                        pltpu.VMEM((nh, CHUNK, CHUNK), jnp.float32)],
        compiler_params=_cparams(("parallel",)),
        name="retention",
    )(z, z, z, z, cosf, sinf, lg, gain.reshape(1, width))


def _gelu_tanh(x):
    return 0.5 * x * (1.0 + jnp.tanh(math.sqrt(2.0 / math.pi) * (x + 0.044715 * (x * x * x))))


def _compress_kernel(k_ref, v_ref, pk_ref, w1k_ref, w2k_ref, pv_ref, w1v_ref, w2v_ref, ko_ref, vo_ref, pad_ref,
                     *, seq):
    nblk = seq // CMP_STRIDE
    for x_ref, p_ref, w1_ref, w2_ref, o_ref, transposed in ((k_ref, pk_ref, w1k_ref, w2k_ref, ko_ref, False),
                                                             (v_ref, pv_ref, w1v_ref, w2v_ref, vo_ref, True)):
        pad_ref[0:seq, :] = x_ref[...]
        pad_ref[seq:seq + CMP_BLOCK, :] = jnp.zeros((CMP_BLOCK, HEAD_DIM), jnp.float32)
        acc = jnp.zeros((nblk, HEAD_DIM), jnp.float32)
        for p in range(CMP_BLOCK):
            xs = pad_ref[pl.ds(p, nblk, stride=CMP_STRIDE), :] + p_ref[p:p + 1, :]
            acc = acc + jnp.dot(xs.astype(jnp.bfloat16), w1_ref[p * HEAD_DIM:(p + 1) * HEAD_DIM, :],
                                preferred_element_type=jnp.float32)
        mid = _gelu_tanh(acc).astype(jnp.bfloat16)
        if transposed:
            out = lax.dot_general(w2_ref[...], mid, _NT, preferred_element_type=jnp.float32)
        else:
            out = jnp.dot(mid, w2_ref[...], preferred_element_type=jnp.float32)
        o_ref[0, 0] = out.astype(o_ref.dtype)


def _compress(z, pos_k, w1_k, w2_k, pos_v, w1_v, w2_v, batch, seq):
    hb = HEAD_DIM
    nblk = seq // CMP_STRIDE
    cmp_in = CMP_BLOCK * hb

    def seg(off):
        return pl.BlockSpec((seq, hb), lambda b, g, off=off: (b, off // hb + g))

    def full(shape):
        return pl.BlockSpec(shape, lambda b, g: (0,) * len(shape))

    out_specs = [pl.BlockSpec((1, 1, nblk, hb), lambda b, g: (b, g, 0, 0)),
                 pl.BlockSpec((1, 1, hb, nblk), lambda b, g: (b, g, 0, 0))]
    out_shape = [jax.ShapeDtypeStruct((batch, NSA_KV_GROUPS, nblk, hb), jnp.bfloat16),
                 jax.ShapeDtypeStruct((batch, NSA_KV_GROUPS, hb, nblk), jnp.bfloat16)]
    return pl.pallas_call(
        functools.partial(_compress_kernel, seq=seq),
        grid=(batch, NSA_KV_GROUPS),
        in_specs=[seg(OFF_NKC), seg(OFF_NVC),
                  full((CMP_BLOCK, hb)), full((cmp_in, hb)), full((hb, hb)),
                  full((CMP_BLOCK, hb)), full((cmp_in, hb)), full((hb, hb))],
        out_specs=out_specs,
        out_shape=out_shape,
        scratch_shapes=[pltpu.VMEM((seq + CMP_BLOCK, hb), jnp.float32)],
        compiler_params=_cparams(("parallel", "parallel")),
        name="nsa_compress",
    )(z, z, pos_k, w1_k, w2_k, pos_v, w1_v, w2_v)


def _rel_bucket_np(dist):
    n = np.maximum(dist, 0)
    exact = REL_BUCKETS // 2
    log_ratio = (np.log(np.maximum(n, 1).astype(np.float32) / np.float32(exact))
                 / np.float32(math.log(REL_MAX_DIST / exact)))
    large = np.minimum(exact + (log_ratio * np.float32(REL_BUCKETS - exact)).astype(np.int32), REL_BUCKETS - 1)
    return np.where(n < exact, n, large).astype(np.int32)


def _bias_kernel(tbl_ref, idxc_ref, idxt_ref, bc_ref, bt_ref):
    h = pl.program_id(0)
    idxc = idxc_ref[...]
    idxt = idxt_ref[...]
    accc = jnp.zeros(idxc.shape, jnp.float32)
    acct = jnp.zeros(idxt.shape, jnp.float32)
    for b in range(REL_BUCKETS):
        val = tbl_ref[b, h]
        accc = jnp.where(idxc == b, val, accc)
        acct = jnp.where(idxt == b, val, acct)
    bc_ref[0] = accc
    bt_ref[:, 0] = acct


def _bias_tables(rel_bias, seq):
    nblk = seq // CMP_STRIDE
    nq = seq // TILE
    j = np.arange(nblk)[:, None]
    t = np.arange(seq)[None, :]
    idx_c = _rel_bucket_np(t - (j * CMP_STRIDE + CMP_BLOCK - 1)).reshape(nblk, nq, TILE).transpose(1, 0, 2)
    kk = np.arange(TILE)[:, None]
    i = np.arange(TILE)[None, :]
    idx_t = np.stack([_rel_bucket_np(i - kk + d * TILE) for d in range(3)])
    return pl.pallas_call(
        _bias_kernel,
        grid=(NSA_HEADS,),
        in_specs=[pl.BlockSpec(memory_space=pltpu.SMEM),
                  pl.BlockSpec((nq, nblk, TILE), lambda h: (0, 0, 0)),
                  pl.BlockSpec((3, TILE, TILE), lambda h: (0, 0, 0))],
        out_specs=[pl.BlockSpec((1, nq, nblk, TILE), lambda h: (h, 0, 0, 0)),
                   pl.BlockSpec((3, 1, TILE, TILE), lambda h: (0, h, 0, 0))],
        out_shape=[jax.ShapeDtypeStruct((NSA_HEADS, nq, nblk, TILE), jnp.float32),
                   jax.ShapeDtypeStruct((3, NSA_HEADS, TILE, TILE), jnp.float32)],
        compiler_params=_cparams(("parallel",)),
        name="nsa_bias_tables",
    )(rel_bias, jnp.asarray(idx_c), jnp.asarray(idx_t))


def _nsa_kernel(q_ref, kc_ref, vct_ref, ks_ref, vs_ref, kw_ref, vw_ref, sm_ref, bc_ref, bt_ref, o_ref,
                kb_ref, vt_ref, *scratch, seq):
    T = TILE
    bf = jnp.bfloat16
    for slot, (k_ref, v_ref) in enumerate(((ks_ref, vs_ref), (kw_ref, vw_ref))):
        for c in range(seq // T):
            kb_ref[slot, c] = k_ref[c * T:(c + 1) * T, :].astype(bf)
            vt_ref[slot, c] = v_ref[c * T:(c + 1) * T, :].T.astype(bf)

    def step(n, carry):
        rows = pl.ds(pl.multiple_of(n * T, T), T)
        _nsa_tile(n, q_ref.at[rows, :], kc_ref, vct_ref, sm_ref.at[rows, :], bc_ref.at[:, n], bt_ref,
                  o_ref.at[rows, :], kb_ref, vt_ref, *scratch, seq=seq)
        return carry

    lax.fori_loop(0, seq // T, step, 0)


def _nsa_tile(n, q_ref, kc_ref, vct_ref, sm_ref, bc_ref, bt_ref, o_ref,
              kb_ref, vt_ref, acc_ref, sw_ref, ss_ref, sel_ref, gate_ref, *, seq):
    R = NSA_GROUP_SIZE
    T = TILE
    nblk = seq // CMP_STRIDE
    n_sel = seq // SEL_BLOCK
    top = min(SEL_TOPK, n_sel)
    bf = jnp.bfloat16
    g = pl.program_id(1)
    t0 = n * T

    q = q_ref[...] * (HEAD_DIM ** -0.5)
    qt = jnp.concatenate([q[:, r * HEAD_DIM:(r + 1) * HEAD_DIM].T for r in range(R)], axis=1).astype(bf)

    def rep(x):
        return jnp.concatenate([x] * R, axis=1)

    j_c = lax.broadcasted_iota(jnp.int32, (nblk, T), 0)
    t_c = t0 + lax.broadcasted_iota(jnp.int32, (nblk, T), 1)
    valid_c = rep(t_c >= j_c * CMP_STRIDE + (CMP_BLOCK - 1))
    s_c = (jnp.dot(kc_ref[0, 0], qt, preferred_element_type=jnp.float32)
           + jnp.concatenate([bc_ref[r] for r in range(R)], axis=1))
    s_c = jnp.where(valid_c, s_c, NEG_INF)
    e_c = jnp.exp(s_c - jnp.max(s_c, axis=0, keepdims=True))
    p_c = jnp.where(valid_c, e_c * (1.0 / jnp.sum(e_c, axis=0, keepdims=True)), 0.0)
    o_c = jnp.dot(vct_ref[0, 0], p_c.astype(bf), preferred_element_type=jnp.float32)

    p_sum = p_c[:, 0:T]
    for r in range(1, R):
        p_sum = p_sum + p_c[:, r * T:(r + 1) * T]
    assert n_sel % 8 == 0 and n_sel <= T
    ov_s = lax.broadcasted_iota(jnp.int32, (n_sel, nblk), 0)
    ov_j = lax.broadcasted_iota(jnp.int32, (n_sel, nblk), 1)
    ratio = SEL_BLOCK // CMP_STRIDE
    span = CMP_BLOCK // CMP_STRIDE
    overlap = ((ov_j < ratio * ov_s + ratio) & (ov_j + span > ratio * ov_s)).astype(jnp.float32)
    imp = jnp.dot(overlap, p_sum, preferred_element_type=jnp.float32, precision=lax.Precision.HIGHEST)

    blk = lax.broadcasted_iota(jnp.int32, (n_sel, T), 0)
    cur = (t0 + lax.broadcasted_iota(jnp.int32, (n_sel, T), 1)) // SEL_BLOCK
    forced = (blk == 0) | (blk == cur) | (blk == cur - 1)
    val = jnp.where(forced, jnp.inf, jnp.where(blk > cur, -jnp.inf, imp))
    blk_f = blk.astype(jnp.float32)
    sel = jnp.zeros((n_sel, T), jnp.float32)
    for _ in range(top):
        best = jnp.max(val, axis=0, keepdims=True)
        first = jnp.min(jnp.where(val == best, blk_f, float(n_sel)), axis=0, keepdims=True)
        hit = blk_f == first
        sel = jnp.where(hit, 1.0, sel)
        val = jnp.where(hit, -jnp.inf, val)
    sel_ref[...] = sel

    rows = lax.broadcasted_iota(jnp.int32, (T, T), 0)
    t_q = t0 + lax.broadcasted_iota(jnp.int32, (T, T), 1)

    def scores(kv, c, allowed):
        mask_add = jnp.where(allowed, 0.0, NEG_INF)
        c_read = jnp.clip(c, 0, n)
        tile_dist = jnp.minimum(n - c_read, 2)
        bias = jnp.concatenate([bt_ref[tile_dist, r] + mask_add for r in range(R)], axis=1)
        return jnp.dot(kb_ref[kv, c_read], qt, preferred_element_type=jnp.float32) + bias

    def col_max(tiles):
        out = jnp.max(tiles[0], axis=0, keepdims=True)
        for s in tiles[1:]:
            out = jnp.maximum(out, jnp.max(s, axis=0, keepdims=True))
        return out

    def weighted_values(kv, tile_ids, probs):
        vt = jnp.concatenate([vt_ref[kv, jnp.clip(c, 0, n)] for c in tile_ids], axis=1)
        total = probs[0].sum(axis=0, keepdims=True)
        for p in probs[1:]:
            total = total + p.sum(axis=0, keepdims=True)
        pv = jnp.dot(vt, jnp.concatenate(probs, axis=0).astype(bf), preferred_element_type=jnp.float32)
        return pv, total

    n_win = WINDOW // T + 1
    win_ids = [n - (n_win - 1) + w for w in range(n_win)]
    for w, c in enumerate(win_ids):
        kpos = c * T + rows
        dist = t_q - kpos
        sw_ref[w] = scores(1, c, (dist >= 0) & (dist < WINDOW) & (kpos >= 0))
    m_w = col_max([sw_ref[w] for w in range(n_win)])
    acc_w, l_w = weighted_values(1, win_ids, [jnp.exp(sw_ref[w] - m_w) for w in range(n_win)])
    o_w = acc_w * (1.0 / l_w)

    blocks_per_tile = T // SEL_BLOCK
    trips = (n + SEL_GROUP) // SEL_GROUP

    def sel_scores(c):
        kpos = c * T + rows
        chosen = sel_ref[pl.ds(c * blocks_per_tile, 1), :]
        for i in range(1, blocks_per_tile):
            chosen = jnp.where(rows < i * SEL_BLOCK, chosen, sel_ref[pl.ds(c * blocks_per_tile + i, 1), :])
        return scores(0, c, (chosen > 0.5) & (kpos <= t_q))

    def sel_pass1(i, m_run):
        ids = [SEL_GROUP * i + u for u in range(SEL_GROUP)]
        tiles = [sel_scores(c) for c in ids]
        for c, s in zip(ids, tiles):
            ss_ref[c] = s
        return jnp.maximum(m_run, col_max(tiles))

    m_s = lax.fori_loop(0, trips, sel_pass1, jnp.full((1, R * T), NEG_INF, jnp.float32))
    acc_ref[...] = jnp.zeros(acc_ref.shape, jnp.float32)

    def sel_pass2(i, l_run):
        ids = [SEL_GROUP * i + u for u in range(SEL_GROUP)]
        pv, total = weighted_values(0, ids, [jnp.exp(ss_ref[c] - m_s) for c in ids])
        acc_ref[...] += pv
        return l_run + total

    l_s = lax.fori_loop(0, trips, sel_pass2, jnp.zeros((1, R * T), jnp.float32))
    o_s = acc_ref[...] * (1.0 / l_s)

    gate_ref[...] = _sigmoid(sm_ref[...]).T
    outs = []
    for r in range(R):
        def gate(branch):
            return gate_ref[pl.ds(SM_GATE + branch * NSA_HEADS + g * R + r, 1), :]
        sl = slice(r * T, (r + 1) * T)
        outs.append((gate(0) * o_c[:, sl] + gate(1) * o_s[:, sl] + gate(2) * o_w[:, sl]).T)
    o_ref[...] = jnp.concatenate(outs, axis=1).astype(o_ref.dtype)


def _nsa(z, zs, k_c, v_c, bias_c, bias_t, batch, seq):
    hb = HEAD_DIM
    R = NSA_GROUP_SIZE
    nblk = seq // CMP_STRIDE
    nq = seq // TILE
    gw = R * hb
    assert OFF_NQ % gw == 0

    def seg(off):
        return pl.BlockSpec((seq, hb), lambda b, g, off=off: (b, off // hb + g))

    return pl.pallas_call(
        functools.partial(_nsa_kernel, seq=seq),
        grid=(batch, NSA_KV_GROUPS),
        in_specs=[pl.BlockSpec((seq, gw), lambda b, g: (b, OFF_NQ // gw + g)),
                  pl.BlockSpec((1, 1, nblk, hb), lambda b, g: (b, g, 0, 0)),
                  pl.BlockSpec((1, 1, hb, nblk), lambda b, g: (b, g, 0, 0)),
                  seg(OFF_NKS), seg(OFF_NVS), seg(OFF_NKW), seg(OFF_NVW),
                  pl.BlockSpec((seq, SMALL_COLS), lambda b, g: (b, 0)),
                  pl.BlockSpec((R, nq, nblk, TILE), lambda b, g: (g, 0, 0, 0)),
                  pl.BlockSpec((3, R, TILE, TILE), lambda b, g: (0, g, 0, 0))],
        out_specs=pl.BlockSpec((seq, gw), lambda b, g: (b, g)),
        out_shape=jax.ShapeDtypeStruct((batch * seq, NSA_WIDTH), jnp.bfloat16),
        scratch_shapes=[pltpu.VMEM((2, nq, TILE, hb), jnp.bfloat16),
                        pltpu.VMEM((2, nq, hb, TILE), jnp.bfloat16),
                        pltpu.VMEM((hb, R * TILE), jnp.float32),
                        pltpu.VMEM((WINDOW // TILE + 1, TILE, R * TILE), jnp.float32),
                        pltpu.VMEM((nq, TILE, R * TILE), jnp.float32),
                        pltpu.VMEM((seq // SEL_BLOCK, TILE), jnp.float32),
                        pltpu.VMEM((SMALL_COLS, TILE), jnp.float32)],
        compiler_params=_cparams(("parallel", "parallel")),
        name="nsa_attention",
    )(z, k_c, v_c, z, z, z, z, zs, bias_c, bias_t)


def _rope_tables(seq):
    half = HEAD_DIM // 2
    inv_freq = 1.0 / (10000.0 ** jnp.linspace(0.0, 1.0, half))
    ang = jnp.arange(seq).astype(jnp.float32)[:, None] * inv_freq[None, :]
    cos, sin = jnp.cos(ang), jnp.sin(ang)
    return jnp.concatenate([cos, cos], axis=-1), jnp.concatenate([-sin, sin], axis=-1)


def _retention_log_decay():
    lg = np.log1p(-np.exp2(-5.0 - np.arange(RET_HEADS, dtype=np.float64))).astype(np.float32)
    return jnp.asarray(np.broadcast_to(lg[:, None, None], (RET_HEADS, 1, HEAD_DIM)).copy())


def _split_in_proj(w_in, b_in, w_gate_down):
    bf = jnp.bfloat16
    a0 = 4 * MLSTM_WIDTH
    a1 = a0 + 2 * MLSTM_HEADS
    a2 = a1 + NSA_WIDTH + 6 * NSA_KV_WIDTH
    a3 = a2 + 3 * NSA_HEADS
    depth, d, _ = w_in.shape
    w_main = jnp.concatenate([w_gate_down.astype(bf), w_in[:, :, :a0].astype(bf), w_in[:, :, a1:a2].astype(bf),
                              w_in[:, :, a3:].astype(bf)], axis=2)
    b_main = jnp.concatenate([jnp.zeros((depth, GATE_RANK), jnp.float32), b_in[:, :a0], b_in[:, a1:a2],
                              b_in[:, a3:]], axis=1)
    n_small = (a1 - a0) + (a3 - a2)
    w_small = jnp.concatenate([w_in[:, :, a0:a1].astype(bf), w_in[:, :, a2:a3].astype(bf),
                               jnp.zeros((depth, d, SMALL_COLS - n_small), bf)], axis=2)
    b_small = jnp.concatenate([b_in[:, a0:a1], b_in[:, a2:a3],
                               jnp.zeros((depth, SMALL_COLS - n_small), jnp.float32)], axis=1)
    return w_main, b_main, w_small, b_small


def kernel(x, rel_bias, norm_mix_g, w_in, b_in, conv_qk, mlstm_norm_g, cmp_pos_k, cmp_w1_k, cmp_w2_k, cmp_pos_v, cmp_w1_v, cmp_w2_v, ret_norm_g, w_br_mlstm, w_br_nsa, w_br_ret, w_gate_down, w_gate_up, b_gate, w_out, norm_mlp_g, w_up, w_down, final_norm_g):
    batch, seq, d = x.shape
    depth = w_in.shape[0]
    bf = jnp.bfloat16
    f32 = jnp.float32
    bias_c, bias_t = _bias_tables(rel_bias, seq)
    cosf, sinf = _rope_tables(seq)
    lg = _retention_log_decay()
    w_main, b_main, w_small, b_small = _split_in_proj(w_in, b_in, w_gate_down)
    w_up_bf = w_up.astype(bf)
    w_down_bf = w_down.astype(bf)
    w1_k, w2_k, w1_v, w2_vt = cmp_w1_k.astype(bf), cmp_w2_k.astype(bf), cmp_w1_v.astype(bf), jnp.swapaxes(cmp_w2_v, 1, 2).astype(bf)
    xr = x.reshape(batch * seq, d)
    for l in range(depth):
        h = _rmsnorm(xr, norm_mix_g[l], bf)
        z = _matmul(h, w_main, l, bias=b_main, out_dtype=f32, tm=1024, tn=1280, tk=D_MODEL, name="in_proj")
        zs = _matmul(h, w_small, l, bias=b_small, out_dtype=f32, tm=1024, tn=SMALL_COLS, tk=D_MODEL,
                     name="in_proj_small")
        y_a = _mlstm(z, zs, conv_qk[l], mlstm_norm_g[l], batch, seq)
        k_c, v_c = _compress(z, cmp_pos_k[l], w1_k[l], w2_k[l], cmp_pos_v[l], w1_v[l], w2_vt[l], batch, seq)
        y_b = _nsa(z, zs, k_c, v_c, bias_c, bias_t, batch, seq)
        y_c = _retention(z, cosf, sinf, lg, ret_norm_g[l], batch, seq)
        merged = _merge(z, y_a, y_b, y_c, l, w_gate_up, b_gate, w_br_mlstm, w_br_nsa, w_br_ret)
        xr = _matmul(merged, w_out, l, res=xr, out_dtype=f32, tm=1024, tn=512, tk=D_MODEL, name="out_proj")
        h = _rmsnorm(xr, norm_mlp_g[l], bf)
        up = _matmul(h, w_up_bf, l, act="relu2", out_dtype=bf, tm=512, tn=2048, tk=D_MODEL, name="mlp_up")
        xr = _matmul(up, w_down_bf, l, res=xr, out_dtype=f32, tm=1024, tn=1024, tk=4096, name="mlp_down")
    return _rmsnorm(xr, final_norm_g, f32).reshape(batch, seq, d)
```

```python
import functools
import math

import numpy as np
import jax
import jax.numpy as jnp
from jax import lax
from jax.experimental import pallas as pl
from jax.experimental.pallas import tpu as pltpu

D_MODEL = 4096
HEAD_DIM = 128
MIX_WIDTH = D_MODEL // 2
MLSTM_HEADS = MIX_WIDTH // (4 * HEAD_DIM)
NSA_HEADS = MIX_WIDTH // (2 * HEAD_DIM)
RET_HEADS = MIX_WIDTH // (4 * HEAD_DIM)
NSA_KV_GROUPS = 2
NSA_GROUP_SIZE = NSA_HEADS // NSA_KV_GROUPS
MLSTM_WIDTH = MLSTM_HEADS * HEAD_DIM
NSA_WIDTH = NSA_HEADS * HEAD_DIM
RET_WIDTH = RET_HEADS * HEAD_DIM
NSA_KV_WIDTH = NSA_KV_GROUPS * HEAD_DIM
CONV_WIDTH = 4
CMP_BLOCK = 32
CMP_STRIDE = 16
SEL_BLOCK = 64
SEL_TOPK = 8
WINDOW = 512
REL_BUCKETS = 32
REL_MAX_DIST = 128
GATE_RANK = D_MODEL // 4
N_BRANCH = 3
EPS = 1e-6
NEG_INF = -1e30

CHUNK = 128
TILE = 128
SEL_GROUP = 4

OFF_GL = 0
OFF_MQ = OFF_GL + GATE_RANK
OFF_MK = OFF_MQ + MLSTM_WIDTH
OFF_MV = OFF_MK + MLSTM_WIDTH
OFF_MO = OFF_MV + MLSTM_WIDTH
OFF_NQ = OFF_MO + MLSTM_WIDTH
OFF_NKC = OFF_NQ + NSA_WIDTH
OFF_NVC = OFF_NKC + NSA_KV_WIDTH
OFF_NKS = OFF_NVC + NSA_KV_WIDTH
OFF_NVS = OFF_NKS + NSA_KV_WIDTH
OFF_NKW = OFF_NVS + NSA_KV_WIDTH
OFF_NVW = OFF_NKW + NSA_KV_WIDTH
OFF_RQ = OFF_NVW + NSA_KV_WIDTH
OFF_RK = OFF_RQ + RET_WIDTH
OFF_RV = OFF_RK + RET_WIDTH
OFF_RG = OFF_RV + RET_WIDTH
MAIN_COLS = OFF_RG + RET_WIDTH
SMALL_COLS = 128
SM_I = 0
SM_F = MLSTM_HEADS
SM_GATE = 2 * MLSTM_HEADS

VMEM_LIMIT = 56 * 1024 * 1024

_NT = (((1,), (1,)), ((), ()))


def _sigmoid(x):
    return 1.0 / (1.0 + jnp.exp(-x))


def _cparams(sem):
    return pltpu.CompilerParams(dimension_semantics=sem, vmem_limit_bytes=VMEM_LIMIT)


def _rmsnorm_kernel(x_ref, g_ref, o_ref):
    x = x_ref[...]
    ms = jnp.mean(x * x, axis=-1, keepdims=True)
    o_ref[...] = (x * lax.rsqrt(ms + EPS) * g_ref[...]).astype(o_ref.dtype)


def _rmsnorm(x, gain, out_dtype, tm=256):
    t, d = x.shape
    return pl.pallas_call(
        _rmsnorm_kernel,
        grid=(t // tm,),
        in_specs=[pl.BlockSpec((tm, d), lambda i: (i, 0)),
                  pl.BlockSpec((1, d), lambda i: (0, 0))],
        out_specs=pl.BlockSpec((tm, d), lambda i: (i, 0)),
        out_shape=jax.ShapeDtypeStruct((t, d), out_dtype),
        compiler_params=_cparams(("parallel",)),
        name="rmsnorm",
    )(x, gain.reshape(1, d))


def _mm_kernel(*refs, nk, act, has_bias, has_res, cast_w, acc_in_out, has_side):
    a_ref, w_ref = refs[0], refs[1]
    pos = 2
    b_ref = refs[pos] if has_bias else None
    pos += int(has_bias)
    r_ref = refs[pos] if has_res else None
    pos += int(has_res)
    side_in_ref = refs[pos] if has_side else None
    pos += int(has_side)
    o_ref = refs[pos]
    pos += 1
    if has_side:
        refs[pos][...] = side_in_ref[...].astype(jnp.bfloat16)
        pos += 1
    has_acc = nk > 1 and not acc_in_out
    acc_ref = refs[pos] if has_acc else None
    pos += int(has_acc)

    if cast_w:
        wb_ref = refs[pos]

        @pl.when(pl.program_id(1) == 0)
        def _():
            wb_ref[...] = w_ref[...].astype(jnp.bfloat16)

        w_ref = wb_ref

    if acc_in_out:
        k = pl.program_id(2)

        @pl.when(k == 0)
        def _():
            o_ref[...] = r_ref[...] if has_res else jnp.zeros(o_ref.shape, o_ref.dtype)

        o_ref[...] += jnp.dot(a_ref[...], w_ref[...], preferred_element_type=jnp.float32)
        return

    part = jnp.dot(a_ref[...], w_ref[...], preferred_element_type=jnp.float32)

    def finish(acc):
        if has_bias:
            acc = acc + b_ref[...]
        if act == "relu2":
            acc = jnp.square(jnp.maximum(acc, 0.0))
        if has_res:
            acc = acc + r_ref[...]
        o_ref[...] = acc.astype(o_ref.dtype)

    if nk == 1:
        finish(part)
    else:
        k = pl.program_id(2)

        @pl.when(k == 0)
        def _():
            acc_ref[...] = part

        @pl.when(k > 0)
        def _():
            acc_ref[...] += part

        @pl.when(k == nk - 1)
        def _():
            finish(acc_ref[...])


def _side_round_specs(side, layer, n_steps, step_id):
    _, rows, cols = side.shape
    assert rows % n_steps == 0 and (rows // n_steps) % 16 == 0
    slab = rows // n_steps
    in_spec = pl.BlockSpec((None, slab, cols), lambda *g: (layer, step_id(*g), 0))
    out_spec = pl.BlockSpec((slab, cols), lambda *g: (step_id(*g), 0))
    return in_spec, out_spec, jax.ShapeDtypeStruct((rows, cols), jnp.bfloat16)


def _matmul(a, w, layer, *, bias=None, res=None, act=None, out_dtype, tm, tn, tk, name, side=None):
    m, kdim = a.shape
    n = w.shape[-1]
    tm, tn, tk = min(tm, m), min(tn, n), min(tk, kdim)
    nk = kdim // tk
    cast_w = w.dtype != jnp.bfloat16
    assert not (cast_w and nk > 1)
    if w.ndim == 3:
        w_spec = pl.BlockSpec((None, tk, tn), lambda j, i, k: (layer, k, j))
    else:
        w_spec = pl.BlockSpec((tk, tn), lambda j, i, k: (k, j))
    in_specs = [pl.BlockSpec((tm, tk), lambda j, i, k: (i, k)), w_spec]
    args = [a, w]
    if bias is not None:
        in_specs.append(pl.BlockSpec((None, 1, tn), lambda j, i, k: (layer, 0, j)))
        args.append(bias.reshape(bias.shape[0], 1, n))
    if res is not None:
        in_specs.append(pl.BlockSpec((tm, tn), lambda j, i, k: (i, j)))
        args.append(res)
    grid = (n // tn, m // tm, nk)
    out_specs = pl.BlockSpec((tm, tn), lambda j, i, k: (i, j))
    out_shape = jax.ShapeDtypeStruct((m, n), out_dtype)
    if side is not None:
        side_in, side_out, side_shape = _side_round_specs(
            side, layer, grid[0] * grid[1] * grid[2], lambda j, i, k: (j * grid[1] + i) * grid[2] + k)
        in_specs.append(side_in)
        args.append(side)
        out_specs, out_shape = [out_specs, side_out], [out_shape, side_shape]
    acc_in_out = nk > 1 and out_dtype == jnp.float32 and act is None and bias is None
    scratch = [pltpu.VMEM((tm, tn), jnp.float32)] if nk > 1 and not acc_in_out else []
    if cast_w:
        scratch.append(pltpu.VMEM((tk, tn), jnp.bfloat16))
    return pl.pallas_call(
        functools.partial(_mm_kernel, nk=nk, act=act, has_bias=bias is not None, has_res=res is not None,
                          cast_w=cast_w, acc_in_out=acc_in_out, has_side=side is not None),
        grid=grid,
        in_specs=in_specs,
        out_specs=out_specs,
        out_shape=out_shape,
        scratch_shapes=scratch,
        compiler_params=_cparams(("parallel", "arbitrary", "arbitrary")),
        name=name,
    )(*args)


def _merge_kernel(gl_ref, ya_ref, yb_ref, yc_ref, wg0_ref, wg1_ref, wg2_ref, bg0_ref, bg1_ref, bg2_ref,
                  wa_ref, wb_ref, wc_ref, o_ref, *wbf_refs):
    @pl.when(pl.program_id(1) == 0)
    def _():
        for src, dst in zip((wg0_ref, wg1_ref, wg2_ref, wa_ref, wb_ref, wc_ref), wbf_refs):
            dst[...] = src[...].astype(jnp.bfloat16)

    wg0, wg1, wg2, wa, wb, wc = wbf_refs
    gl = gl_ref[...].astype(jnp.bfloat16)

    def branch(wg, bg_ref, y_ref, w):
        gate = _sigmoid(jnp.dot(gl, wg[...], preferred_element_type=jnp.float32) + bg_ref[...])
        return gate * jnp.dot(y_ref[...], w[...], preferred_element_type=jnp.float32)

    out = branch(wg0, bg0_ref, ya_ref, wa) + branch(wg1, bg1_ref, yb_ref, wb) + branch(wg2, bg2_ref, yc_ref, wc)
    o_ref[...] = out.astype(o_ref.dtype)


def _merge(z, ya, yb, yc, layer, w_gate_up, b_gate, wa, wb, wc, tm=1024, tn=512):
    t = z.shape[0]
    d = wa.shape[2]
    tm = min(tm, t)
    nj = d // tn
    assert OFF_GL % GATE_RANK == 0
    gl_blk = OFF_GL // GATE_RANK

    def wg_spec(i):
        return pl.BlockSpec((None, GATE_RANK, tn), lambda j, m, i=i: (layer, 0, i * nj + j))

    def bg_spec(i):
        return pl.BlockSpec((None, 1, tn), lambda j, m, i=i: (layer, 0, i * nj + j))

    def row_spec(width):
        return pl.BlockSpec((tm, width), lambda j, m: (m, 0))

    def w_spec(width):
        return pl.BlockSpec((None, width, tn), lambda j, m: (layer, 0, j))

    bg = b_gate.reshape(b_gate.shape[0], 1, N_BRANCH * d)
    bf_scratch = [pltpu.VMEM((rows, tn), jnp.bfloat16)
                  for rows in (GATE_RANK, GATE_RANK, GATE_RANK, MLSTM_WIDTH, NSA_WIDTH, RET_WIDTH)]
    return pl.pallas_call(
        _merge_kernel,
        grid=(nj, t // tm),
        in_specs=[pl.BlockSpec((tm, GATE_RANK), lambda j, m: (m, gl_blk)),
                  row_spec(MLSTM_WIDTH), row_spec(NSA_WIDTH), row_spec(RET_WIDTH),
                  wg_spec(0), wg_spec(1), wg_spec(2), bg_spec(0), bg_spec(1), bg_spec(2),
                  w_spec(MLSTM_WIDTH), w_spec(NSA_WIDTH), w_spec(RET_WIDTH)],
        out_specs=pl.BlockSpec((tm, tn), lambda j, m: (m, j)),
        out_shape=jax.ShapeDtypeStruct((t, d), jnp.bfloat16),
        scratch_shapes=bf_scratch,
        compiler_params=_cparams(("parallel", "arbitrary")),
        name="gated_merge",
    )(z, ya, yb, yc, w_gate_up, w_gate_up, w_gate_up, bg, bg, bg, wa, wb, wc)


def _head_norm(h, gain):
    mu = jnp.mean(h, axis=-1, keepdims=True)
    d = h - mu
    var = jnp.mean(d * d, axis=-1, keepdims=True)
    return d * lax.rsqrt(var + EPS) * gain


def _mlstm_kernel(q_ref, k_ref, v_ref, og_ref, sm_ref, cq_ref, ck_ref, gain_ref, o_ref, *, n_chunks):
    L = CHUNK
    h = pl.program_id(1)
    lane_ids = lax.broadcasted_iota(jnp.int32, (L, SMALL_COLS), 1)
    row = lax.broadcasted_iota(jnp.int32, (L, L), 0)
    col = lax.broadcasted_iota(jnp.int32, (L, L), 1)
    causal = col <= row
    eye = col == row
    cq = cq_ref[...]
    ck = ck_ref[...]
    gain = gain_ref[...]
    scale = HEAD_DIM ** -0.5

    def conv_silu(ref, w, c, start):
        cur = ref[pl.ds(start, L), :]
        pstart = pl.multiple_of(jnp.maximum(start - 8, 0), 8)
        prev = jnp.where(c > 0, ref[pl.ds(pstart, 8), :], 0.0)
        ext = jnp.concatenate([prev, cur], axis=0)
        y = w[CONV_WIDTH - 1:CONV_WIDTH, :] * cur
        for kk in range(CONV_WIDTH - 1):
            off = 8 - (CONV_WIDTH - 1) + kk
            y = y + w[kk:kk + 1, :] * ext[off:off + L, :]
        return y * _sigmoid(y)

    def lane_pick(x, lane):
        return jnp.sum(jnp.where(lane_ids == lane, x, 0.0), axis=1, keepdims=True)

    def body(c, carry):
        cmat, nvec, m = carry
        start = pl.multiple_of(c * L, L)
        q = conv_silu(q_ref, cq, c, start)
        k = conv_silu(k_ref, ck, c, start) * scale
        v = v_ref[pl.ds(start, L), :]
        sm = sm_ref[pl.ds(start, L), :]
        i_col = lane_pick(sm, SM_I + h)
        f_col = lane_pick(sm, SM_F + h)
        ls_col = jnp.minimum(f_col, 0.0) - jnp.log(1.0 + jnp.exp(-jnp.abs(f_col)))
        a_row = jnp.sum(jnp.where(row <= col, ls_col, 0.0), axis=0, keepdims=True)
        a_col = jnp.sum(jnp.where(eye, a_row, 0.0), axis=1, keepdims=True)
        i_row = jnp.sum(jnp.where(eye, i_col, 0.0), axis=0, keepdims=True)
        g = jnp.sum(ls_col, axis=0, keepdims=True)
        w_col = g - a_col + i_col
        m_loc = jnp.max(w_col, axis=0, keepdims=True)
        e_col = jnp.exp(w_col - m_loc)

        log_d = jnp.where(causal, a_col - a_row + i_row, NEG_INF)
        log_inter = a_col + m
        m_row = jnp.maximum(log_inter, jnp.max(log_d, axis=1, keepdims=True))
        inter = jnp.exp(log_inter - m_row)
        dmat = jnp.exp(log_d - m_row)

        qb = q.astype(jnp.bfloat16)
        kb = k.astype(jnp.bfloat16)
        vb = v.astype(jnp.bfloat16)
        sc = lax.dot_general(qb, kb, _NT, preferred_element_type=jnp.float32) * dmat
        num = (inter * jnp.dot(qb, cmat.astype(jnp.bfloat16), preferred_element_type=jnp.float32)
               + jnp.dot(sc.astype(jnp.bfloat16), vb, preferred_element_type=jnp.float32))
        den = inter * jnp.sum(q * nvec, axis=1, keepdims=True) + jnp.sum(sc, axis=1, keepdims=True)
        hout = num / jnp.maximum(jnp.abs(den), jnp.exp(-m_row))

        y = _head_norm(hout, gain)
        o_ref[pl.ds(start, L), :] = (_sigmoid(og_ref[pl.ds(start, L), :]) * y).astype(o_ref.dtype)

        m_new = jnp.maximum(g + m, m_loc)
        a_old = jnp.exp(g + m - m_new)
        a_new = jnp.exp(m_loc - m_new)
        ke = k * e_col
        c_new = a_old * cmat + a_new * jnp.dot(ke.T.astype(jnp.bfloat16), vb, preferred_element_type=jnp.float32)
        n_new = a_old * nvec + a_new * jnp.sum(ke, axis=0, keepdims=True)
        return c_new, n_new, m_new

    init = (jnp.zeros((HEAD_DIM, HEAD_DIM), jnp.float32), jnp.zeros((1, HEAD_DIM), jnp.float32),
            jnp.zeros((1, 1), jnp.float32))
    lax.fori_loop(0, n_chunks, body, init)


def _mlstm(z, zs, conv_qk, gain, batch, seq):
    nh = MLSTM_HEADS
    hb = HEAD_DIM

    def seg(off):
        return pl.BlockSpec((seq, hb), lambda b, h, off=off: (b, off // hb + h))

    return pl.pallas_call(
        functools.partial(_mlstm_kernel, n_chunks=seq // CHUNK),
        grid=(batch, nh),
        in_specs=[seg(OFF_MQ), seg(OFF_MK), seg(OFF_MV), seg(OFF_MO),
                  pl.BlockSpec((seq, SMALL_COLS), lambda b, h: (b, 0)),
                  pl.BlockSpec((CONV_WIDTH, hb), lambda b, h: (0, h)),
                  pl.BlockSpec((CONV_WIDTH, hb), lambda b, h: (0, nh + h)),
                  pl.BlockSpec((1, hb), lambda b, h: (0, h))],
        out_specs=pl.BlockSpec((seq, hb), lambda b, h: (b, h)),
        out_shape=jax.ShapeDtypeStruct((batch * seq, MLSTM_WIDTH), jnp.bfloat16),
        compiler_params=_cparams(("parallel", "parallel")),
        name="mlstm",
    )(z, z, z, z, zs, conv_qk, conv_qk, gain.reshape(1, MLSTM_WIDTH))


def _retention_kernel(q_ref, k_ref, v_ref, g_ref, cos_ref, sin_ref, lg_ref, gain_ref, o_ref, r_ref, decay_ref,
                      *, n_chunks):
    L = CHUNK
    H = RET_HEADS
    D = HEAD_DIM
    bf = jnp.bfloat16
    row = lax.broadcasted_iota(jnp.int32, (L, L), 0)
    col = lax.broadcasted_iota(jnp.int32, (L, L), 1)
    rel = (row - col).astype(jnp.float32)
    idx = lax.broadcasted_iota(jnp.int32, (L, 1), 0).astype(jnp.float32)
    gain = gain_ref[...]
    scale = HEAD_DIM ** -0.5
    lgs = [lg_ref[h][:, 0:1] for h in range(H)]
    for h in range(H):
        decay_ref[h] = jnp.where(rel >= 0, jnp.exp(lgs[h] * jnp.maximum(rel, 0.0)), 0.0)
    r_ref[...] = jnp.zeros(r_ref.shape, jnp.float32)

    def body(c, carry):
        start = pl.multiple_of(c * L, L)
        cosf = cos_ref[pl.ds(start, L), :]
        sinf = sin_ref[pl.ds(start, L), :]
        q_all = q_ref[pl.ds(start, L), :]
        k_all = k_ref[pl.ds(start, L), :]
        v_all = v_ref[pl.ds(start, L), :]
        heads = []
        for h in range(H):
            sl = slice(h * D, (h + 1) * D)

            def rot(x):
                return x * cosf + pltpu.roll(x, D // 2, 1) * sinf

            k = rot(k_all[:, sl]) * scale
            heads.append(dict(sl=sl, qb=rot(q_all[:, sl]).astype(bf), kb=k.astype(bf), vb=v_all[:, sl].astype(bf),
                              kz=k * jnp.exp(lgs[h] * (L - 1.0 - idx))))
        for h, s in enumerate(heads):
            s["qk"] = lax.dot_general(s["qb"], s["kb"], _NT, preferred_element_type=jnp.float32)
            s["qr"] = jnp.dot(s["qb"], r_ref[h].astype(bf), preferred_element_type=jnp.float32)
            s["kv"] = jnp.dot(s["kz"].T.astype(bf), s["vb"], preferred_element_type=jnp.float32)
        for h, s in enumerate(heads):
            s["sv"] = jnp.dot((s["qk"] * decay_ref[h]).astype(bf), s["vb"], preferred_element_type=jnp.float32)
        outs = []
        for h, s in enumerate(heads):
            out = s["sv"] + jnp.exp(lgs[h] * (idx + 1.0)) * s["qr"]
            outs.append(_head_norm(out, gain[:, s["sl"]]))
            r_ref[h] = jnp.exp(lgs[h] * L) * r_ref[h] + s["kv"]
        gate = g_ref[pl.ds(start, L), :]
        o_ref[pl.ds(start, L), :] = (gate * _sigmoid(gate) * jnp.concatenate(outs, axis=1)).astype(o_ref.dtype)
        return carry

    lax.fori_loop(0, n_chunks, body, 0)


def _retention(z, cosf, sinf, lg, gain, batch, seq):
    nh = RET_HEADS
    hb = HEAD_DIM
    width = RET_WIDTH

    def seg(off):
        assert off % width == 0
        return pl.BlockSpec((seq, width), lambda b, off=off: (b, off // width))

    table = pl.BlockSpec((seq, hb), lambda b: (0, 0))
    return pl.pallas_call(
        functools.partial(_retention_kernel, n_chunks=seq // CHUNK),
        grid=(batch,),
        in_specs=[seg(OFF_RQ), seg(OFF_RK), seg(OFF_RV), seg(OFF_RG), table, table,
                  pl.BlockSpec((nh, 1, hb), lambda b: (0, 0, 0)),
                  pl.BlockSpec((1, width), lambda b: (0, 0))],
        out_specs=pl.BlockSpec((seq, width), lambda b: (b, 0)),
        out_shape=jax.ShapeDtypeStruct((batch * seq, width), jnp.bfloat16),
        scratch_shapes=[pltpu.VMEM((nh, hb, hb), jnp.float32),
                        pltpu.VMEM((nh, CHUNK, CHUNK), jnp.float32)],
        compiler_params=_cparams(("parallel",)),
        name="retention",
    )(z, z, z, z, cosf, sinf, lg, gain.reshape(1, width))


def _gelu_tanh(x):
    return 0.5 * x * (1.0 + jnp.tanh(math.sqrt(2.0 / math.pi) * (x + 0.044715 * (x * x * x))))


def _compress_kernel(k_ref, v_ref, pk_ref, w1k_ref, w2k_ref, pv_ref, w1v_ref, w2v_ref, ko_ref, vo_ref, pad_ref,
                     *, seq):
    nblk = seq // CMP_STRIDE
    for x_ref, p_ref, w1_ref, w2_ref, o_ref, transposed in ((k_ref, pk_ref, w1k_ref, w2k_ref, ko_ref, False),
                                                             (v_ref, pv_ref, w1v_ref, w2v_ref, vo_ref, True)):
        pad_ref[0:seq, :] = x_ref[...]
        pad_ref[seq:seq + CMP_BLOCK, :] = jnp.zeros((CMP_BLOCK, HEAD_DIM), jnp.float32)
        acc = jnp.zeros((nblk, HEAD_DIM), jnp.float32)
        for p in range(CMP_BLOCK):
            xs = pad_ref[pl.ds(p, nblk, stride=CMP_STRIDE), :] + p_ref[p:p + 1, :]
            acc = acc + jnp.dot(xs.astype(jnp.bfloat16), w1_ref[p * HEAD_DIM:(p + 1) * HEAD_DIM, :],
                                preferred_element_type=jnp.float32)
        mid = _gelu_tanh(acc).astype(jnp.bfloat16)
        if transposed:
            out = lax.dot_general(w2_ref[...], mid, _NT, preferred_element_type=jnp.float32)
        else:
            out = jnp.dot(mid, w2_ref[...], preferred_element_type=jnp.float32)
        o_ref[0, 0] = out.astype(o_ref.dtype)


def _compress(z, pos_k, w1_k, w2_k, pos_v, w1_v, w2_v, batch, seq):
    hb = HEAD_DIM
    nblk = seq // CMP_STRIDE
    cmp_in = CMP_BLOCK * hb

    def seg(off):
        return pl.BlockSpec((seq, hb), lambda b, g, off=off: (b, off // hb + g))

    def full(shape):
        return pl.BlockSpec(shape, lambda b, g: (0,) * len(shape))

    out_specs = [pl.BlockSpec((1, 1, nblk, hb), lambda b, g: (b, g, 0, 0)),
                 pl.BlockSpec((1, 1, hb, nblk), lambda b, g: (b, g, 0, 0))]
    out_shape = [jax.ShapeDtypeStruct((batch, NSA_KV_GROUPS, nblk, hb), jnp.bfloat16),
                 jax.ShapeDtypeStruct((batch, NSA_KV_GROUPS, hb, nblk), jnp.bfloat16)]
    return pl.pallas_call(
        functools.partial(_compress_kernel, seq=seq),
        grid=(batch, NSA_KV_GROUPS),
        in_specs=[seg(OFF_NKC), seg(OFF_NVC),
                  full((CMP_BLOCK, hb)), full((cmp_in, hb)), full((hb, hb)),
                  full((CMP_BLOCK, hb)), full((cmp_in, hb)), full((hb, hb))],
        out_specs=out_specs,
        out_shape=out_shape,
        scratch_shapes=[pltpu.VMEM((seq + CMP_BLOCK, hb), jnp.float32)],
        compiler_params=_cparams(("parallel", "parallel")),
        name="nsa_compress",
    )(z, z, pos_k, w1_k, w2_k, pos_v, w1_v, w2_v)


def _rel_bucket_np(dist):
    n = np.maximum(dist, 0)
    exact = REL_BUCKETS // 2
    log_ratio = (np.log(np.maximum(n, 1).astype(np.float32) / np.float32(exact))
                 / np.float32(math.log(REL_MAX_DIST / exact)))
    large = np.minimum(exact + (log_ratio * np.float32(REL_BUCKETS - exact)).astype(np.int32), REL_BUCKETS - 1)
    return np.where(n < exact, n, large).astype(np.int32)


def _bias_kernel(tbl_ref, idxc_ref, idxt_ref, bc_ref, bt_ref):
    h = pl.program_id(0)
    idxc = idxc_ref[...]
    idxt = idxt_ref[...]
    accc = jnp.zeros(idxc.shape, jnp.float32)
    acct = jnp.zeros(idxt.shape, jnp.float32)
    for b in range(REL_BUCKETS):
        val = tbl_ref[b, h]
        accc = jnp.where(idxc == b, val, accc)
        acct = jnp.where(idxt == b, val, acct)
    bc_ref[0] = accc
    bt_ref[:, 0] = acct


def _bias_tables(rel_bias, seq):
    nblk = seq // CMP_STRIDE
    nq = seq // TILE
    j = np.arange(nblk)[:, None]
    t = np.arange(seq)[None, :]
    idx_c = _rel_bucket_np(t - (j * CMP_STRIDE + CMP_BLOCK - 1)).reshape(nblk, nq, TILE).transpose(1, 0, 2)
    kk = np.arange(TILE)[:, None]
    i = np.arange(TILE)[None, :]
    idx_t = np.stack([_rel_bucket_np(i - kk + d * TILE) for d in range(3)])
    return pl.pallas_call(
        _bias_kernel,
        grid=(NSA_HEADS,),
        in_specs=[pl.BlockSpec(memory_space=pltpu.SMEM),
                  pl.BlockSpec((nq, nblk, TILE), lambda h: (0, 0, 0)),
                  pl.BlockSpec((3, TILE, TILE), lambda h: (0, 0, 0))],
        out_specs=[pl.BlockSpec((1, nq, nblk, TILE), lambda h: (h, 0, 0, 0)),
                   pl.BlockSpec((3, 1, TILE, TILE), lambda h: (0, h, 0, 0))],
        out_shape=[jax.ShapeDtypeStruct((NSA_HEADS, nq, nblk, TILE), jnp.float32),
                   jax.ShapeDtypeStruct((3, NSA_HEADS, TILE, TILE), jnp.float32)],
        compiler_params=_cparams(("parallel",)),
        name="nsa_bias_tables",
    )(rel_bias, jnp.asarray(idx_c), jnp.asarray(idx_t))


def _nsa_kernel(q_ref, kc_ref, vct_ref, ks_ref, vs_ref, kw_ref, vw_ref, sm_ref, bc_ref, bt_ref, o_ref,
                kb_ref, vt_ref, *scratch, seq):
    T = TILE
    bf = jnp.bfloat16
    for slot, (k_ref, v_ref) in enumerate(((ks_ref, vs_ref), (kw_ref, vw_ref))):
        for c in range(seq // T):
            kb_ref[slot, c] = k_ref[c * T:(c + 1) * T, :].astype(bf)
            vt_ref[slot, c] = v_ref[c * T:(c + 1) * T, :].T.astype(bf)

    def step(n, carry):
        rows = pl.ds(pl.multiple_of(n * T, T), T)
        _nsa_tile(n, q_ref.at[rows, :], kc_ref, vct_ref, sm_ref.at[rows, :], bc_ref.at[:, n], bt_ref,
                  o_ref.at[rows, :], kb_ref, vt_ref, *scratch, seq=seq)
        return carry

    lax.fori_loop(0, seq // T, step, 0)


def _nsa_tile(n, q_ref, kc_ref, vct_ref, sm_ref, bc_ref, bt_ref, o_ref,
              kb_ref, vt_ref, acc_ref, sw_ref, ss_ref, sel_ref, gate_ref, *, seq):
    R = NSA_GROUP_SIZE
    T = TILE
    nblk = seq // CMP_STRIDE
    n_sel = seq // SEL_BLOCK
    top = min(SEL_TOPK, n_sel)
    bf = jnp.bfloat16
    g = pl.program_id(1)
    t0 = n * T

    q = q_ref[...] * (HEAD_DIM ** -0.5)
    qt = jnp.concatenate([q[:, r * HEAD_DIM:(r + 1) * HEAD_DIM].T for r in range(R)], axis=1).astype(bf)

    def rep(x):
        return jnp.concatenate([x] * R, axis=1)

    j_c = lax.broadcasted_iota(jnp.int32, (nblk, T), 0)
    t_c = t0 + lax.broadcasted_iota(jnp.int32, (nblk, T), 1)
    valid_c = rep(t_c >= j_c * CMP_STRIDE + (CMP_BLOCK - 1))
    s_c = (jnp.dot(kc_ref[0, 0], qt, preferred_element_type=jnp.float32)
           + jnp.concatenate([bc_ref[r] for r in range(R)], axis=1))
    s_c = jnp.where(valid_c, s_c, NEG_INF)
    e_c = jnp.exp(s_c - jnp.max(s_c, axis=0, keepdims=True))
    p_c = jnp.where(valid_c, e_c * (1.0 / jnp.sum(e_c, axis=0, keepdims=True)), 0.0)
    o_c = jnp.dot(vct_ref[0, 0], p_c.astype(bf), preferred_element_type=jnp.float32)

    p_sum = p_c[:, 0:T]
    for r in range(1, R):
        p_sum = p_sum + p_c[:, r * T:(r + 1) * T]
    assert n_sel % 8 == 0 and n_sel <= T
    ov_s = lax.broadcasted_iota(jnp.int32, (n_sel, nblk), 0)
    ov_j = lax.broadcasted_iota(jnp.int32, (n_sel, nblk), 1)
    ratio = SEL_BLOCK // CMP_STRIDE
    span = CMP_BLOCK // CMP_STRIDE
    overlap = ((ov_j < ratio * ov_s + ratio) & (ov_j + span > ratio * ov_s)).astype(jnp.float32)
    imp = jnp.dot(overlap, p_sum, preferred_element_type=jnp.float32, precision=lax.Precision.HIGHEST)

    blk = lax.broadcasted_iota(jnp.int32, (n_sel, T), 0)
    cur = (t0 + lax.broadcasted_iota(jnp.int32, (n_sel, T), 1)) // SEL_BLOCK
    forced = (blk == 0) | (blk == cur) | (blk == cur - 1)
    val = jnp.where(forced, jnp.inf, jnp.where(blk > cur, -jnp.inf, imp))
    blk_f = blk.astype(jnp.float32)
    sel = jnp.zeros((n_sel, T), jnp.float32)
    for _ in range(top):
        best = jnp.max(val, axis=0, keepdims=True)
        first = jnp.min(jnp.where(val == best, blk_f, float(n_sel)), axis=0, keepdims=True)
        hit = blk_f == first
        sel = jnp.where(hit, 1.0, sel)
        val = jnp.where(hit, -jnp.inf, val)
    sel_ref[...] = sel

    rows = lax.broadcasted_iota(jnp.int32, (T, T), 0)
    t_q = t0 + lax.broadcasted_iota(jnp.int32, (T, T), 1)

    def scores(kv, c, allowed):
        mask_add = jnp.where(allowed, 0.0, NEG_INF)
        c_read = jnp.clip(c, 0, n)
        tile_dist = jnp.minimum(n - c_read, 2)
        bias = jnp.concatenate([bt_ref[tile_dist, r] + mask_add for r in range(R)], axis=1)
        return jnp.dot(kb_ref[kv, c_read], qt, preferred_element_type=jnp.float32) + bias

    def col_max(tiles):
        out = jnp.max(tiles[0], axis=0, keepdims=True)
        for s in tiles[1:]:
            out = jnp.maximum(out, jnp.max(s, axis=0, keepdims=True))
        return out

    def weighted_values(kv, tile_ids, probs):
        vt = jnp.concatenate([vt_ref[kv, jnp.clip(c, 0, n)] for c in tile_ids], axis=1)
        total = probs[0].sum(axis=0, keepdims=True)
        for p in probs[1:]:
            total = total + p.sum(axis=0, keepdims=True)
        pv = jnp.dot(vt, jnp.concatenate(probs, axis=0).astype(bf), preferred_element_type=jnp.float32)
        return pv, total

    n_win = WINDOW // T + 1
    win_ids = [n - (n_win - 1) + w for w in range(n_win)]
    for w, c in enumerate(win_ids):
        kpos = c * T + rows
        dist = t_q - kpos
        sw_ref[w] = scores(1, c, (dist >= 0) & (dist < WINDOW) & (kpos >= 0))
    m_w = col_max([sw_ref[w] for w in range(n_win)])
    acc_w, l_w = weighted_values(1, win_ids, [jnp.exp(sw_ref[w] - m_w) for w in range(n_win)])
    o_w = acc_w * (1.0 / l_w)

    blocks_per_tile = T // SEL_BLOCK
    trips = (n + SEL_GROUP) // SEL_GROUP

    def sel_scores(c):
        kpos = c * T + rows
        chosen = sel_ref[pl.ds(c * blocks_per_tile, 1), :]
        for i in range(1, blocks_per_tile):
            chosen = jnp.where(rows < i * SEL_BLOCK, chosen, sel_ref[pl.ds(c * blocks_per_tile + i, 1), :])
        return scores(0, c, (chosen > 0.5) & (kpos <= t_q))

    def sel_pass1(i, m_run):
        ids = [SEL_GROUP * i + u for u in range(SEL_GROUP)]
        tiles = [sel_scores(c) for c in ids]
        for c, s in zip(ids, tiles):
            ss_ref[c] = s
        return jnp.maximum(m_run, col_max(tiles))

    m_s = lax.fori_loop(0, trips, sel_pass1, jnp.full((1, R * T), NEG_INF, jnp.float32))
    acc_ref[...] = jnp.zeros(acc_ref.shape, jnp.float32)

    def sel_pass2(i, l_run):
        ids = [SEL_GROUP * i + u for u in range(SEL_GROUP)]
        pv, total = weighted_values(0, ids, [jnp.exp(ss_ref[c] - m_s) for c in ids])
        acc_ref[...] += pv
        return l_run + total

    l_s = lax.fori_loop(0, trips, sel_pass2, jnp.zeros((1, R * T), jnp.float32))
    o_s = acc_ref[...] * (1.0 / l_s)

    gate_ref[...] = _sigmoid(sm_ref[...]).T
    outs = []
    for r in range(R):
        def gate(branch):
            return gate_ref[pl.ds(SM_GATE + branch * NSA_HEADS + g * R + r, 1), :]
        sl = slice(r * T, (r + 1) * T)
        outs.append((gate(0) * o_c[:, sl] + gate(1) * o_s[:, sl] + gate(2) * o_w[:, sl]).T)
    o_ref[...] = jnp.concatenate(outs, axis=1).astype(o_ref.dtype)


def _nsa(z, zs, k_c, v_c, bias_c, bias_t, batch, seq):
    hb = HEAD_DIM
    R = NSA_GROUP_SIZE
    nblk = seq // CMP_STRIDE
    nq = seq // TILE
    gw = R * hb
    assert OFF_NQ % gw == 0

    def seg(off):
        return pl.BlockSpec((seq, hb), lambda b, g, off=off: (b, off // hb + g))

    return pl.pallas_call(
        functools.partial(_nsa_kernel, seq=seq),
        grid=(batch, NSA_KV_GROUPS),
        in_specs=[pl.BlockSpec((seq, gw), lambda b, g: (b, OFF_NQ // gw + g)),
                  pl.BlockSpec((1, 1, nblk, hb), lambda b, g: (b, g, 0, 0)),
                  pl.BlockSpec((1, 1, hb, nblk), lambda b, g: (b, g, 0, 0)),
                  seg(OFF_NKS), seg(OFF_NVS), seg(OFF_NKW), seg(OFF_NVW),
                  pl.BlockSpec((seq, SMALL_COLS), lambda b, g: (b, 0)),
                  pl.BlockSpec((R, nq, nblk, TILE), lambda b, g: (g, 0, 0, 0)),
                  pl.BlockSpec((3, R, TILE, TILE), lambda b, g: (0, g, 0, 0))],
        out_specs=pl.BlockSpec((seq, gw), lambda b, g: (b, g)),
        out_shape=jax.ShapeDtypeStruct((batch * seq, NSA_WIDTH), jnp.bfloat16),
        scratch_shapes=[pltpu.VMEM((2, nq, TILE, hb), jnp.bfloat16),
                        pltpu.VMEM((2, nq, hb, TILE), jnp.bfloat16),
                        pltpu.VMEM((hb, R * TILE), jnp.float32),
                        pltpu.VMEM((WINDOW // TILE + 1, TILE, R * TILE), jnp.float32),
                        pltpu.VMEM((nq, TILE, R * TILE), jnp.float32),
                        pltpu.VMEM((seq // SEL_BLOCK, TILE), jnp.float32),
                        pltpu.VMEM((SMALL_COLS, TILE), jnp.float32)],
        compiler_params=_cparams(("parallel", "parallel")),
        name="nsa_attention",
    )(z, k_c, v_c, z, z, z, z, zs, bias_c, bias_t)


def _rope_tables(seq):
    half = HEAD_DIM // 2
    inv_freq = 1.0 / (10000.0 ** jnp.linspace(0.0, 1.0, half))
    ang = jnp.arange(seq).astype(jnp.float32)[:, None] * inv_freq[None, :]
    cos, sin = jnp.cos(ang), jnp.sin(ang)
    return jnp.concatenate([cos, cos], axis=-1), jnp.concatenate([-sin, sin], axis=-1)


def _retention_log_decay():
    lg = np.log1p(-np.exp2(-5.0 - np.arange(RET_HEADS, dtype=np.float64))).astype(np.float32)
    return jnp.asarray(np.broadcast_to(lg[:, None, None], (RET_HEADS, 1, HEAD_DIM)).copy())


def _split_in_proj(w_in, b_in, w_gate_down):
    bf = jnp.bfloat16
    a0 = 4 * MLSTM_WIDTH
    a1 = a0 + 2 * MLSTM_HEADS
    a2 = a1 + NSA_WIDTH + 6 * NSA_KV_WIDTH
    a3 = a2 + 3 * NSA_HEADS
    depth, d, _ = w_in.shape
    w_main = jnp.concatenate([w_gate_down.astype(bf), w_in[:, :, :a0].astype(bf), w_in[:, :, a1:a2].astype(bf),
                              w_in[:, :, a3:].astype(bf)], axis=2)
    b_main = jnp.concatenate([jnp.zeros((depth, GATE_RANK), jnp.float32), b_in[:, :a0], b_in[:, a1:a2],
                              b_in[:, a3:]], axis=1)
    n_small = (a1 - a0) + (a3 - a2)
    w_small = jnp.concatenate([w_in[:, :, a0:a1].astype(bf), w_in[:, :, a2:a3].astype(bf),
                               jnp.zeros((depth, d, SMALL_COLS - n_small), bf)], axis=2)
    b_small = jnp.concatenate([b_in[:, a0:a1], b_in[:, a2:a3],
                               jnp.zeros((depth, SMALL_COLS - n_small), jnp.float32)], axis=1)
    return w_main, b_main, w_small, b_small


def kernel(x, rel_bias, norm_mix_g, w_in, b_in, conv_qk, mlstm_norm_g, cmp_pos_k, cmp_w1_k, cmp_w2_k, cmp_pos_v, cmp_w1_v, cmp_w2_v, ret_norm_g, w_br_mlstm, w_br_nsa, w_br_ret, w_gate_down, w_gate_up, b_gate, w_out, norm_mlp_g, w_up, w_down, final_norm_g):
    batch, seq, d = x.shape
    depth = w_in.shape[0]
    bf = jnp.bfloat16
    f32 = jnp.float32
    bias_c, bias_t = _bias_tables(rel_bias, seq)
    cosf, sinf = _rope_tables(seq)
    lg = _retention_log_decay()
    w_main, b_main, w_small, b_small = _split_in_proj(w_in, b_in, w_gate_down)
    w1_k, w2_k, w1_v, w2_vt = cmp_w1_k.astype(bf), cmp_w2_k.astype(bf), cmp_w1_v.astype(bf), jnp.swapaxes(cmp_w2_v, 1, 2).astype(bf)
    xr = x.reshape(batch * seq, d)
    for l in range(depth):
        h = _rmsnorm(xr, norm_mix_g[l], bf)
        z = _matmul(h, w_main, l, bias=b_main, out_dtype=f32, tm=1024, tn=1280, tk=D_MODEL, name="in_proj")
        zs = _matmul(h, w_small, l, bias=b_small, out_dtype=f32, tm=1024, tn=SMALL_COLS, tk=D_MODEL,
                     name="in_proj_small")
        y_a = _mlstm(z, zs, conv_qk[l], mlstm_norm_g[l], batch, seq)
        k_c, v_c = _compress(z, cmp_pos_k[l], w1_k[l], w2_k[l], cmp_pos_v[l], w1_v[l], w2_vt[l], batch, seq)
        y_b = _nsa(z, zs, k_c, v_c, bias_c, bias_t, batch, seq)
        y_c = _retention(z, cosf, sinf, lg, ret_norm_g[l], batch, seq)
        merged = _merge(z, y_a, y_b, y_c, l, w_gate_up, b_gate, w_br_mlstm, w_br_nsa, w_br_ret)
        xr, w_up_l = _matmul(merged, w_out, l, res=xr, out_dtype=f32, tm=1024, tn=512, tk=D_MODEL, name="out_proj",
                             side=w_up)
        h = _rmsnorm(xr, norm_mlp_g[l], bf)
        up, w_down_l = _matmul(h, w_up_l, l, act="relu2", out_dtype=bf, tm=512, tn=2048, tk=D_MODEL, name="mlp_up",
                               side=w_down)
        xr = _matmul(up, w_down_l, None, res=xr, out_dtype=f32, tm=1024, tn=1024, tk=4096, name="mlp_down")
    return _rmsnorm(xr, final_norm_g, f32).reshape(batch, seq, d)
```

```python
import functools
import math

import numpy as np
import jax
import jax.numpy as jnp
from jax import lax
from jax.experimental import pallas as pl
from jax.experimental.pallas import tpu as pltpu

D_MODEL = 4096
HEAD_DIM = 128
MIX_WIDTH = D_MODEL // 2
MLSTM_HEADS = MIX_WIDTH // (4 * HEAD_DIM)
NSA_HEADS = MIX_WIDTH // (2 * HEAD_DIM)
RET_HEADS = MIX_WIDTH // (4 * HEAD_DIM)
NSA_KV_GROUPS = 2
NSA_GROUP_SIZE = NSA_HEADS // NSA_KV_GROUPS
MLSTM_WIDTH = MLSTM_HEADS * HEAD_DIM
NSA_WIDTH = NSA_HEADS * HEAD_DIM
RET_WIDTH = RET_HEADS * HEAD_DIM
NSA_KV_WIDTH = NSA_KV_GROUPS * HEAD_DIM
CONV_WIDTH = 4
CMP_BLOCK = 32
CMP_STRIDE = 16
SEL_BLOCK = 64
SEL_TOPK = 8
WINDOW = 512
REL_BUCKETS = 32
REL_MAX_DIST = 128
GATE_RANK = D_MODEL // 4
N_BRANCH = 3
EPS = 1e-6
NEG_INF = -1e30

CHUNK = 128
TILE = 128
SEL_GROUP = 4

OFF_GL = 0
OFF_MQ = OFF_GL + GATE_RANK
OFF_MK = OFF_MQ + MLSTM_WIDTH
OFF_MV = OFF_MK + MLSTM_WIDTH
OFF_MO = OFF_MV + MLSTM_WIDTH
OFF_NQ = OFF_MO + MLSTM_WIDTH
OFF_NKC = OFF_NQ + NSA_WIDTH
OFF_NVC = OFF_NKC + NSA_KV_WIDTH
OFF_NKS = OFF_NVC + NSA_KV_WIDTH
OFF_NVS = OFF_NKS + NSA_KV_WIDTH
OFF_NKW = OFF_NVS + NSA_KV_WIDTH
OFF_NVW = OFF_NKW + NSA_KV_WIDTH
OFF_RQ = OFF_NVW + NSA_KV_WIDTH
OFF_RK = OFF_RQ + RET_WIDTH
OFF_RV = OFF_RK + RET_WIDTH
OFF_RG = OFF_RV + RET_WIDTH
MAIN_COLS = OFF_RG + RET_WIDTH
SMALL_COLS = 128
SM_I = 0
SM_F = MLSTM_HEADS
SM_GATE = 2 * MLSTM_HEADS

VMEM_LIMIT = 56 * 1024 * 1024

_NT = (((1,), (1,)), ((), ()))


def _sigmoid(x):
    return 1.0 / (1.0 + jnp.exp(-x))


def _cparams(sem):
    return pltpu.CompilerParams(dimension_semantics=sem, vmem_limit_bytes=VMEM_LIMIT)


def _rmsnorm_kernel(x_ref, g_ref, o_ref):
    x = x_ref[...]
    ms = jnp.mean(x * x, axis=-1, keepdims=True)
    o_ref[...] = (x * lax.rsqrt(ms + EPS) * g_ref[...]).astype(o_ref.dtype)


def _rmsnorm(x, gain, out_dtype, tm=256):
    t, d = x.shape
    return pl.pallas_call(
        _rmsnorm_kernel,
        grid=(t // tm,),
        in_specs=[pl.BlockSpec((tm, d), lambda i: (i, 0)),
                  pl.BlockSpec((1, d), lambda i: (0, 0))],
        out_specs=pl.BlockSpec((tm, d), lambda i: (i, 0)),
        out_shape=jax.ShapeDtypeStruct((t, d), out_dtype),
        compiler_params=_cparams(("parallel",)),
        name="rmsnorm",
    )(x, gain.reshape(1, d))


def _mm_kernel(*refs, nk, act, has_bias, has_res, cast_w, acc_in_out, has_side):
    a_ref, w_ref = refs[0], refs[1]
    pos = 2
    b_ref = refs[pos] if has_bias else None
    pos += int(has_bias)
    r_ref = refs[pos] if has_res else None
    pos += int(has_res)
    side_in_ref = refs[pos] if has_side else None
    pos += int(has_side)
    o_ref = refs[pos]
    pos += 1
    if has_side:
        refs[pos][...] = side_in_ref[...].astype(jnp.bfloat16)
        pos += 1
    has_acc = nk > 1 and not acc_in_out
    acc_ref = refs[pos] if has_acc else None
    pos += int(has_acc)

    if cast_w:
        wb_ref = refs[pos]

        @pl.when(pl.program_id(1) == 0)
        def _():
            wb_ref[...] = w_ref[...].astype(jnp.bfloat16)

        w_ref = wb_ref

    if acc_in_out:
        k = pl.program_id(2)

        @pl.when(k == 0)
        def _():
            o_ref[...] = r_ref[...] if has_res else jnp.zeros(o_ref.shape, o_ref.dtype)

        o_ref[...] += jnp.dot(a_ref[...], w_ref[...], preferred_element_type=jnp.float32)
        return

    part = jnp.dot(a_ref[...], w_ref[...], preferred_element_type=jnp.float32)

    def finish(acc):
        if has_bias:
            acc = acc + b_ref[...]
        if act == "relu2":
            acc = jnp.square(jnp.maximum(acc, 0.0))
        if has_res:
            acc = acc + r_ref[...]
        o_ref[...] = acc.astype(o_ref.dtype)

    if nk == 1:
        finish(part)
    else:
        k = pl.program_id(2)

        @pl.when(k == 0)
        def _():
            acc_ref[...] = part

        @pl.when(k > 0)
        def _():
            acc_ref[...] += part

        @pl.when(k == nk - 1)
        def _():
            finish(acc_ref[...])


def _side_round_specs(side, layer, n_steps, step_id):
    _, rows, cols = side.shape
    assert rows % n_steps == 0 and (rows // n_steps) % 16 == 0
    slab = rows // n_steps
    in_spec = pl.BlockSpec((None, slab, cols), lambda *g: (layer, step_id(*g), 0))
    out_spec = pl.BlockSpec((slab, cols), lambda *g: (step_id(*g), 0))
    return in_spec, out_spec, jax.ShapeDtypeStruct((rows, cols), jnp.bfloat16)


def _matmul(a, w, layer, *, bias=None, res=None, act=None, out_dtype, tm, tn, tk, name, side=None):
    m, kdim = a.shape
    n = w.shape[-1]
    tm, tn, tk = min(tm, m), min(tn, n), min(tk, kdim)
    nk = kdim // tk
    cast_w = w.dtype != jnp.bfloat16
    assert not (cast_w and nk > 1)
    if w.ndim == 3:
        w_spec = pl.BlockSpec((None, tk, tn), lambda j, i, k: (layer, k, j))
    else:
        w_spec = pl.BlockSpec((tk, tn), lambda j, i, k: (k, j))
    in_specs = [pl.BlockSpec((tm, tk), lambda j, i, k: (i, k)), w_spec]
    args = [a, w]
    if bias is not None:
        in_specs.append(pl.BlockSpec((None, 1, tn), lambda j, i, k: (layer, 0, j)))
        args.append(bias.reshape(bias.shape[0], 1, n))
    if res is not None:
        in_specs.append(pl.BlockSpec((tm, tn), lambda j, i, k: (i, j)))
        args.append(res)
    grid = (n // tn, m // tm, nk)
    out_specs = pl.BlockSpec((tm, tn), lambda j, i, k: (i, j))
    out_shape = jax.ShapeDtypeStruct((m, n), out_dtype)
    if side is not None:
        side_in, side_out, side_shape = _side_round_specs(
            side, layer, grid[0] * grid[1] * grid[2], lambda j, i, k: (j * grid[1] + i) * grid[2] + k)
        in_specs.append(side_in)
        args.append(side)
        out_specs, out_shape = [out_specs, side_out], [out_shape, side_shape]
    acc_in_out = nk > 1 and out_dtype == jnp.float32 and act is None and bias is None
    scratch = [pltpu.VMEM((tm, tn), jnp.float32)] if nk > 1 and not acc_in_out else []
    if cast_w:
        scratch.append(pltpu.VMEM((tk, tn), jnp.bfloat16))
    return pl.pallas_call(
        functools.partial(_mm_kernel, nk=nk, act=act, has_bias=bias is not None, has_res=res is not None,
                          cast_w=cast_w, acc_in_out=acc_in_out, has_side=side is not None),
        grid=grid,
        in_specs=in_specs,
        out_specs=out_specs,
        out_shape=out_shape,
        scratch_shapes=scratch,
        compiler_params=_cparams(("parallel", "arbitrary", "arbitrary")),
        name=name,
    )(*args)


def _merge_kernel(gl_ref, ya_ref, yb_ref, yc_ref, wg0_ref, wg1_ref, wg2_ref, bg0_ref, bg1_ref, bg2_ref,
                  wa_ref, wb_ref, wc_ref, o_ref, *wbf_refs):
    @pl.when(pl.program_id(1) == 0)
    def _():
        for src, dst in zip((wg0_ref, wg1_ref, wg2_ref, wa_ref, wb_ref, wc_ref), wbf_refs):
            dst[...] = src[...].astype(jnp.bfloat16)

    wg0, wg1, wg2, wa, wb, wc = wbf_refs
    gl = gl_ref[...].astype(jnp.bfloat16)

    def branch(wg, bg_ref, y_ref, w):
        gate = _sigmoid(jnp.dot(gl, wg[...], preferred_element_type=jnp.float32) + bg_ref[...])
        return gate * jnp.dot(y_ref[...], w[...], preferred_element_type=jnp.float32)

    out = branch(wg0, bg0_ref, ya_ref, wa) + branch(wg1, bg1_ref, yb_ref, wb) + branch(wg2, bg2_ref, yc_ref, wc)
    o_ref[...] = out.astype(o_ref.dtype)


def _merge(z, ya, yb, yc, layer, w_gate_up, b_gate, wa, wb, wc, tm=1024, tn=512):
    t = z.shape[0]
    d = wa.shape[2]
    tm = min(tm, t)
    nj = d // tn
    assert OFF_GL % GATE_RANK == 0
    gl_blk = OFF_GL // GATE_RANK

    def wg_spec(i):
        return pl.BlockSpec((None, GATE_RANK, tn), lambda j, m, i=i: (layer, 0, i * nj + j))

    def bg_spec(i):
        return pl.BlockSpec((None, 1, tn), lambda j, m, i=i: (layer, 0, i * nj + j))

    def row_spec(width):
        return pl.BlockSpec((tm, width), lambda j, m: (m, 0))

    def w_spec(width):
        return pl.BlockSpec((None, width, tn), lambda j, m: (layer, 0, j))

    bg = b_gate.reshape(b_gate.shape[0], 1, N_BRANCH * d)
    bf_scratch = [pltpu.VMEM((rows, tn), jnp.bfloat16)
                  for rows in (GATE_RANK, GATE_RANK, GATE_RANK, MLSTM_WIDTH, NSA_WIDTH, RET_WIDTH)]
    return pl.pallas_call(
        _merge_kernel,
        grid=(nj, t // tm),
        in_specs=[pl.BlockSpec((tm, GATE_RANK), lambda j, m: (m, gl_blk)),
                  row_spec(MLSTM_WIDTH), row_spec(NSA_WIDTH), row_spec(RET_WIDTH),
                  wg_spec(0), wg_spec(1), wg_spec(2), bg_spec(0), bg_spec(1), bg_spec(2),
                  w_spec(MLSTM_WIDTH), w_spec(NSA_WIDTH), w_spec(RET_WIDTH)],
        out_specs=pl.BlockSpec((tm, tn), lambda j, m: (m, j)),
        out_shape=jax.ShapeDtypeStruct((t, d), jnp.bfloat16),
        scratch_shapes=bf_scratch,
        compiler_params=_cparams(("parallel", "arbitrary")),
        name="gated_merge",
    )(z, ya, yb, yc, w_gate_up, w_gate_up, w_gate_up, bg, bg, bg, wa, wb, wc)


def _head_norm(h, gain):
    mu = jnp.mean(h, axis=-1, keepdims=True)
    d = h - mu
    var = jnp.mean(d * d, axis=-1, keepdims=True)
    return d * lax.rsqrt(var + EPS) * gain


def _mlstm_kernel(q_ref, k_ref, v_ref, og_ref, sm_ref, cq_ref, ck_ref, gain_ref, o_ref, *, n_chunks):
    L = CHUNK
    h = pl.program_id(1)
    lane_ids = lax.broadcasted_iota(jnp.int32, (L, SMALL_COLS), 1)
    row = lax.broadcasted_iota(jnp.int32, (L, L), 0)
    col = lax.broadcasted_iota(jnp.int32, (L, L), 1)
    causal = col <= row
    eye = col == row
    cq = cq_ref[...]
    ck = ck_ref[...]
    gain = gain_ref[...]
    scale = HEAD_DIM ** -0.5

    def conv_silu(ref, w, c, start):
        cur = ref[pl.ds(start, L), :]
        pstart = pl.multiple_of(jnp.maximum(start - 8, 0), 8)
        prev = jnp.where(c > 0, ref[pl.ds(pstart, 8), :], 0.0)
        ext = jnp.concatenate([prev, cur], axis=0)
        y = w[CONV_WIDTH - 1:CONV_WIDTH, :] * cur
        for kk in range(CONV_WIDTH - 1):
            off = 8 - (CONV_WIDTH - 1) + kk
            y = y + w[kk:kk + 1, :] * ext[off:off + L, :]
        return y * _sigmoid(y)

    def lane_pick(x, lane):
        return jnp.sum(jnp.where(lane_ids == lane, x, 0.0), axis=1, keepdims=True)

    def body(c, carry):
        cmat, nvec, m = carry
        start = pl.multiple_of(c * L, L)
        q = conv_silu(q_ref, cq, c, start)
        k = conv_silu(k_ref, ck, c, start) * scale
        v = v_ref[pl.ds(start, L), :]
        sm = sm_ref[pl.ds(start, L), :]
        i_col = lane_pick(sm, SM_I + h)
        f_col = lane_pick(sm, SM_F + h)
        ls_col = jnp.minimum(f_col, 0.0) - jnp.log(1.0 + jnp.exp(-jnp.abs(f_col)))
        a_row = jnp.sum(jnp.where(row <= col, ls_col, 0.0), axis=0, keepdims=True)
        a_col = jnp.sum(jnp.where(eye, a_row, 0.0), axis=1, keepdims=True)
        i_row = jnp.sum(jnp.where(eye, i_col, 0.0), axis=0, keepdims=True)
        g = jnp.sum(ls_col, axis=0, keepdims=True)
        w_col = g - a_col + i_col
        m_loc = jnp.max(w_col, axis=0, keepdims=True)
        e_col = jnp.exp(w_col - m_loc)

        log_d = jnp.where(causal, a_col - a_row + i_row, NEG_INF)
        log_inter = a_col + m
        m_row = jnp.maximum(log_inter, jnp.max(log_d, axis=1, keepdims=True))
        inter = jnp.exp(log_inter - m_row)
        dmat = jnp.exp(log_d - m_row)

        qb = q.astype(jnp.bfloat16)
        kb = k.astype(jnp.bfloat16)
        vb = v.astype(jnp.bfloat16)
        sc = lax.dot_general(qb, kb, _NT, preferred_element_type=jnp.float32) * dmat
        num = (inter * jnp.dot(qb, cmat.astype(jnp.bfloat16), preferred_element_type=jnp.float32)
               + jnp.dot(sc.astype(jnp.bfloat16), vb, preferred_element_type=jnp.float32))
        den = inter * jnp.sum(q * nvec, axis=1, keepdims=True) + jnp.sum(sc, axis=1, keepdims=True)
        hout = num / jnp.maximum(jnp.abs(den), jnp.exp(-m_row))

        y = _head_norm(hout, gain)
        o_ref[pl.ds(start, L), :] = (_sigmoid(og_ref[pl.ds(start, L), :]) * y).astype(o_ref.dtype)

        m_new = jnp.maximum(g + m, m_loc)
        a_old = jnp.exp(g + m - m_new)
        a_new = jnp.exp(m_loc - m_new)
        ke = k * e_col
        c_new = a_old * cmat + a_new * jnp.dot(ke.T.astype(jnp.bfloat16), vb, preferred_element_type=jnp.float32)
        n_new = a_old * nvec + a_new * jnp.sum(ke, axis=0, keepdims=True)
        return c_new, n_new, m_new

    init = (jnp.zeros((HEAD_DIM, HEAD_DIM), jnp.float32), jnp.zeros((1, HEAD_DIM), jnp.float32),
            jnp.zeros((1, 1), jnp.float32))
    lax.fori_loop(0, n_chunks, body, init)


def _mlstm(z, zs, conv_qk, gain, batch, seq):
    nh = MLSTM_HEADS
    hb = HEAD_DIM

    def seg(off):
        return pl.BlockSpec((seq, hb), lambda b, h, off=off: (b, off // hb + h))

    return pl.pallas_call(
        functools.partial(_mlstm_kernel, n_chunks=seq // CHUNK),
        grid=(batch, nh),
        in_specs=[seg(OFF_MQ), seg(OFF_MK), seg(OFF_MV), seg(OFF_MO),
                  pl.BlockSpec((seq, SMALL_COLS), lambda b, h: (b, 0)),
                  pl.BlockSpec((CONV_WIDTH, hb), lambda b, h: (0, h)),
                  pl.BlockSpec((CONV_WIDTH, hb), lambda b, h: (0, nh + h)),
                  pl.BlockSpec((1, hb), lambda b, h: (0, h))],
        out_specs=pl.BlockSpec((seq, hb), lambda b, h: (b, h)),
        out_shape=jax.ShapeDtypeStruct((batch * seq, MLSTM_WIDTH), jnp.bfloat16),
        compiler_params=_cparams(("parallel", "parallel")),
        name="mlstm",
    )(z, z, z, z, zs, conv_qk, conv_qk, gain.reshape(1, MLSTM_WIDTH))


def _retention_kernel(q_ref, k_ref, v_ref, g_ref, cos_ref, sin_ref, lg_ref, gain_ref, o_ref, r_ref, decay_ref,
                      *, n_chunks):
    L = CHUNK
    H = RET_HEADS
    D = HEAD_DIM
    bf = jnp.bfloat16
    row = lax.broadcasted_iota(jnp.int32, (L, L), 0)
    col = lax.broadcasted_iota(jnp.int32, (L, L), 1)
    rel = (row - col).astype(jnp.float32)
    idx = lax.broadcasted_iota(jnp.int32, (L, 1), 0).astype(jnp.float32)
    gain = gain_ref[...]
    scale = HEAD_DIM ** -0.5
    lgs = [lg_ref[h][:, 0:1] for h in range(H)]
    for h in range(H):
        decay_ref[h] = jnp.where(rel >= 0, jnp.exp(lgs[h] * jnp.maximum(rel, 0.0)), 0.0)
    r_ref[...] = jnp.zeros(r_ref.shape, jnp.float32)

    def body(c, carry):
        start = pl.multiple_of(c * L, L)
        cosf = cos_ref[pl.ds(start, L), :]
        sinf = sin_ref[pl.ds(start, L), :]
        q_all = q_ref[pl.ds(start, L), :]
        k_all = k_ref[pl.ds(start, L), :]
        v_all = v_ref[pl.ds(start, L), :]
        heads = []
        for h in range(H):
            sl = slice(h * D, (h + 1) * D)

            def rot(x):
                return x * cosf + pltpu.roll(x, D // 2, 1) * sinf

            k = rot(k_all[:, sl]) * scale
            heads.append(dict(sl=sl, qb=rot(q_all[:, sl]).astype(bf), kb=k.astype(bf), vb=v_all[:, sl].astype(bf),
                              kz=k * jnp.exp(lgs[h] * (L - 1.0 - idx))))
        for h, s in enumerate(heads):
            s["qk"] = lax.dot_general(s["qb"], s["kb"], _NT, preferred_element_type=jnp.float32)
            s["qr"] = jnp.dot(s["qb"], r_ref[h].astype(bf), preferred_element_type=jnp.float32)
            s["kv"] = jnp.dot(s["kz"].T.astype(bf), s["vb"], preferred_element_type=jnp.float32)
        for h, s in enumerate(heads):
            s["sv"] = jnp.dot((s["qk"] * decay_ref[h]).astype(bf), s["vb"], preferred_element_type=jnp.float32)
        outs = []
        for h, s in enumerate(heads):
            out = s["sv"] + jnp.exp(lgs[h] * (idx + 1.0)) * s["qr"]
            outs.append(_head_norm(out, gain[:, s["sl"]]))
            r_ref[h] = jnp.exp(lgs[h] * L) * r_ref[h] + s["kv"]
        gate = g_ref[pl.ds(start, L), :]
        o_ref[pl.ds(start, L), :] = (gate * _sigmoid(gate) * jnp.concatenate(outs, axis=1)).astype(o_ref.dtype)
        return carry

    lax.fori_loop(0, n_chunks, body, 0)


def _retention(z, cosf, sinf, lg, gain, batch, seq):
    nh = RET_HEADS
    hb = HEAD_DIM
    width = RET_WIDTH

    def seg(off):
        assert off % width == 0
        return pl.BlockSpec((seq, width), lambda b, off=off: (b, off // width))

    table = pl.BlockSpec((seq, hb), lambda b: (0, 0))
    return pl.pallas_call(
        functools.partial(_retention_kernel, n_chunks=seq // CHUNK),
        grid=(batch,),
        in_specs=[seg(OFF_RQ), seg(OFF_RK), seg(OFF_RV), seg(OFF_RG), table, table,
                  pl.BlockSpec((nh, 1, hb), lambda b: (0, 0, 0)),
                  pl.BlockSpec((1, width), lambda b: (0, 0))],
        out_specs=pl.BlockSpec((seq, width), lambda b: (b, 0)),
        out_shape=jax.ShapeDtypeStruct((batch * seq, width), jnp.bfloat16),
        scratch_shapes=[pltpu.VMEM((nh, hb, hb), jnp.float32),
                        pltpu.VMEM((nh, CHUNK, CHUNK), jnp.float32)],
        compiler_params=_cparams(("parallel",)),
        name="retention",
    )(z, z, z, z, cosf, sinf, lg, gain.reshape(1, width))


def _gelu_tanh(x):
    return 0.5 * x * (1.0 + jnp.tanh(math.sqrt(2.0 / math.pi) * (x + 0.044715 * (x * x * x))))


def _compress_kernel(k_ref, v_ref, pk_ref, w1k_ref, w2k_ref, pv_ref, w1v_ref, w2v_ref, ko_ref, vo_ref, pad_ref,
                     *, seq):
    nblk = seq // CMP_STRIDE
    for x_ref, p_ref, w1_ref, w2_ref, o_ref, transposed in ((k_ref, pk_ref, w1k_ref, w2k_ref, ko_ref, False),
                                                             (v_ref, pv_ref, w1v_ref, w2v_ref, vo_ref, True)):
        pad_ref[0:seq, :] = x_ref[...]
        pad_ref[seq:seq + CMP_BLOCK, :] = jnp.zeros((CMP_BLOCK, HEAD_DIM), jnp.float32)
        acc = jnp.zeros((nblk, HEAD_DIM), jnp.float32)
        for p in range(CMP_BLOCK):
            xs = pad_ref[pl.ds(p, nblk, stride=CMP_STRIDE), :] + p_ref[p:p + 1, :]
            acc = acc + jnp.dot(xs.astype(jnp.bfloat16), w1_ref[p * HEAD_DIM:(p + 1) * HEAD_DIM, :],
                                preferred_element_type=jnp.float32)
        mid = _gelu_tanh(acc).astype(jnp.bfloat16)
        if transposed:
            out = lax.dot_general(w2_ref[...], mid, _NT, preferred_element_type=jnp.float32)
        else:
            out = jnp.dot(mid, w2_ref[...], preferred_element_type=jnp.float32)
        o_ref[0, 0] = out.astype(o_ref.dtype)


def _compress(z, pos_k, w1_k, w2_k, pos_v, w1_v, w2_v, batch, seq):
    hb = HEAD_DIM
    nblk = seq // CMP_STRIDE
    cmp_in = CMP_BLOCK * hb

    def seg(off):
        return pl.BlockSpec((seq, hb), lambda b, g, off=off: (b, off // hb + g))

    def full(shape):
        return pl.BlockSpec(shape, lambda b, g: (0,) * len(shape))

    out_specs = [pl.BlockSpec((1, 1, nblk, hb), lambda b, g: (b, g, 0, 0)),
                 pl.BlockSpec((1, 1, hb, nblk), lambda b, g: (b, g, 0, 0))]
    out_shape = [jax.ShapeDtypeStruct((batch, NSA_KV_GROUPS, nblk, hb), jnp.bfloat16),
                 jax.ShapeDtypeStruct((batch, NSA_KV_GROUPS, hb, nblk), jnp.bfloat16)]
    return pl.pallas_call(
        functools.partial(_compress_kernel, seq=seq),
        grid=(batch, NSA_KV_GROUPS),
        in_specs=[seg(OFF_NKC), seg(OFF_NVC),
                  full((CMP_BLOCK, hb)), full((cmp_in, hb)), full((hb, hb)),
                  full((CMP_BLOCK, hb)), full((cmp_in, hb)), full((hb, hb))],
        out_specs=out_specs,
        out_shape=out_shape,
        scratch_shapes=[pltpu.VMEM((seq + CMP_BLOCK, hb), jnp.float32)],
        compiler_params=_cparams(("parallel", "parallel")),
        name="nsa_compress",
    )(z, z, pos_k, w1_k, w2_k, pos_v, w1_v, w2_v)


def _rel_bucket_np(dist):
    n = np.maximum(dist, 0)
    exact = REL_BUCKETS // 2
    log_ratio = (np.log(np.maximum(n, 1).astype(np.float32) / np.float32(exact))
                 / np.float32(math.log(REL_MAX_DIST / exact)))
    large = np.minimum(exact + (log_ratio * np.float32(REL_BUCKETS - exact)).astype(np.int32), REL_BUCKETS - 1)
    return np.where(n < exact, n, large).astype(np.int32)


def _bias_kernel(tbl_ref, idxc_ref, idxt_ref, bc_ref, bt_ref):
    h = pl.program_id(0)
    idxc = idxc_ref[...]
    idxt = idxt_ref[...]
    accc = jnp.zeros(idxc.shape, jnp.float32)
    acct = jnp.zeros(idxt.shape, jnp.float32)
    for b in range(REL_BUCKETS):
        val = tbl_ref[b, h]
        accc = jnp.where(idxc == b, val, accc)
        acct = jnp.where(idxt == b, val, acct)
    bc_ref[0] = accc
    bt_ref[:, 0] = acct


def _bias_tables(rel_bias, seq):
    nblk = seq // CMP_STRIDE
    nq = seq // TILE
    j = np.arange(nblk)[:, None]
    t = np.arange(seq)[None, :]
    idx_c = _rel_bucket_np(t - (j * CMP_STRIDE + CMP_BLOCK - 1)).reshape(nblk, nq, TILE).transpose(1, 0, 2)
    kk = np.arange(TILE)[:, None]
    i = np.arange(TILE)[None, :]
    idx_t = np.stack([_rel_bucket_np(i - kk + d * TILE) for d in range(3)])
    return pl.pallas_call(
        _bias_kernel,
        grid=(NSA_HEADS,),
        in_specs=[pl.BlockSpec(memory_space=pltpu.SMEM),
                  pl.BlockSpec((nq, nblk, TILE), lambda h: (0, 0, 0)),
                  pl.BlockSpec((3, TILE, TILE), lambda h: (0, 0, 0))],
        out_specs=[pl.BlockSpec((1, nq, nblk, TILE), lambda h: (h, 0, 0, 0)),
                   pl.BlockSpec((3, 1, TILE, TILE), lambda h: (0, h, 0, 0))],
        out_shape=[jax.ShapeDtypeStruct((NSA_HEADS, nq, nblk, TILE), jnp.float32),
                   jax.ShapeDtypeStruct((3, NSA_HEADS, TILE, TILE), jnp.float32)],
        compiler_params=_cparams(("parallel",)),
        name="nsa_bias_tables",
    )(rel_bias, jnp.asarray(idx_c), jnp.asarray(idx_t))


def _nsa_kernel(q_ref, kc_ref, vct_ref, ks_ref, vs_ref, kw_ref, vw_ref, sm_ref, bc_ref, bt_ref, o_ref,
                kb_ref, vt_ref, *scratch, seq):
    T = TILE
    bf = jnp.bfloat16
    for slot, (k_ref, v_ref) in enumerate(((ks_ref, vs_ref), (kw_ref, vw_ref))):
        for c in range(seq // T):
            kb_ref[slot, c] = k_ref[c * T:(c + 1) * T, :].astype(bf)
            vt_ref[slot, c] = v_ref[c * T:(c + 1) * T, :].T.astype(bf)

    def step(n, carry):
        rows = pl.ds(pl.multiple_of(n * T, T), T)
        _nsa_tile(n, q_ref.at[rows, :], kc_ref, vct_ref, sm_ref.at[rows, :], bc_ref.at[:, n], bt_ref,
                  o_ref.at[rows, :], kb_ref, vt_ref, *scratch, seq=seq)
        return carry

    lax.fori_loop(0, seq // T, step, 0)


def _nsa_tile(n, q_ref, kc_ref, vct_ref, sm_ref, bc_ref, bt_ref, o_ref,
              kb_ref, vt_ref, acc_ref, sw_ref, ss_ref, sel_ref, gate_ref, *, seq):
    R = NSA_GROUP_SIZE
    T = TILE
    nblk = seq // CMP_STRIDE
    n_sel = seq // SEL_BLOCK
    top = min(SEL_TOPK, n_sel)
    bf = jnp.bfloat16
    g = pl.program_id(1)
    t0 = n * T

    q = q_ref[...] * (HEAD_DIM ** -0.5)
    qt = jnp.concatenate([q[:, r * HEAD_DIM:(r + 1) * HEAD_DIM].T for r in range(R)], axis=1).astype(bf)

    def rep(x):
        return jnp.concatenate([x] * R, axis=1)

    j_c = lax.broadcasted_iota(jnp.int32, (nblk, T), 0)
    t_c = t0 + lax.broadcasted_iota(jnp.int32, (nblk, T), 1)
    valid_c = rep(t_c >= j_c * CMP_STRIDE + (CMP_BLOCK - 1))
    s_c = (jnp.dot(kc_ref[0, 0], qt, preferred_element_type=jnp.float32)
           + jnp.concatenate([bc_ref[r] for r in range(R)], axis=1))
    s_c = jnp.where(valid_c, s_c, NEG_INF)
    e_c = jnp.exp(s_c - jnp.max(s_c, axis=0, keepdims=True))
    p_c = jnp.where(valid_c, e_c * (1.0 / jnp.sum(e_c, axis=0, keepdims=True)), 0.0)
    o_c = jnp.dot(vct_ref[0, 0], p_c.astype(bf), preferred_element_type=jnp.float32)

    p_sum = p_c[:, 0:T]
    for r in range(1, R):
        p_sum = p_sum + p_c[:, r * T:(r + 1) * T]
    assert n_sel % 8 == 0 and n_sel <= T
    ov_s = lax.broadcasted_iota(jnp.int32, (n_sel, nblk), 0)
    ov_j = lax.broadcasted_iota(jnp.int32, (n_sel, nblk), 1)
    ratio = SEL_BLOCK // CMP_STRIDE
    span = CMP_BLOCK // CMP_STRIDE
    overlap = ((ov_j < ratio * ov_s + ratio) & (ov_j + span > ratio * ov_s)).astype(jnp.float32)
    imp = jnp.dot(overlap, p_sum, preferred_element_type=jnp.float32, precision=lax.Precision.HIGHEST)

    blk = lax.broadcasted_iota(jnp.int32, (n_sel, T), 0)
    cur = (t0 + lax.broadcasted_iota(jnp.int32, (n_sel, T), 1)) // SEL_BLOCK
    forced = (blk == 0) | (blk == cur) | (blk == cur - 1)
    val = jnp.where(forced, jnp.inf, jnp.where(blk > cur, -jnp.inf, imp))
    blk_f = blk.astype(jnp.float32)
    sel = jnp.zeros((n_sel, T), jnp.float32)
    for _ in range(top):
        best = jnp.max(val, axis=0, keepdims=True)
        first = jnp.min(jnp.where(val == best, blk_f, float(n_sel)), axis=0, keepdims=True)
        hit = blk_f == first
        sel = jnp.where(hit, 1.0, sel)
        val = jnp.where(hit, -jnp.inf, val)
    sel_ref[...] = sel

    rows = lax.broadcasted_iota(jnp.int32, (T, T), 0)
    t_q = t0 + lax.broadcasted_iota(jnp.int32, (T, T), 1)

    def scores(kv, c, allowed):
        mask_add = jnp.where(allowed, 0.0, NEG_INF)
        c_read = jnp.clip(c, 0, n)
        tile_dist = jnp.minimum(n - c_read, 2)
        bias = jnp.concatenate([bt_ref[tile_dist, r] + mask_add for r in range(R)], axis=1)
        return jnp.dot(kb_ref[kv, c_read], qt, preferred_element_type=jnp.float32) + bias

    def col_max(tiles):
        out = jnp.max(tiles[0], axis=0, keepdims=True)
        for s in tiles[1:]:
            out = jnp.maximum(out, jnp.max(s, axis=0, keepdims=True))
        return out

    def weighted_values(kv, tile_ids, probs):
        vt = jnp.concatenate([vt_ref[kv, jnp.clip(c, 0, n)] for c in tile_ids], axis=1)
        total = probs[0].sum(axis=0, keepdims=True)
        for p in probs[1:]:
            total = total + p.sum(axis=0, keepdims=True)
        pv = jnp.dot(vt, jnp.concatenate(probs, axis=0).astype(bf), preferred_element_type=jnp.float32)
        return pv, total

    n_win = WINDOW // T + 1
    win_ids = [n - (n_win - 1) + w for w in range(n_win)]
    for w, c in enumerate(win_ids):
        kpos = c * T + rows
        dist = t_q - kpos
        sw_ref[w] = scores(1, c, (dist >= 0) & (dist < WINDOW) & (kpos >= 0))
    m_w = col_max([sw_ref[w] for w in range(n_win)])
    acc_w, l_w = weighted_values(1, win_ids, [jnp.exp(sw_ref[w] - m_w) for w in range(n_win)])
    o_w = acc_w * (1.0 / l_w)

    blocks_per_tile = T // SEL_BLOCK
    trips = (n + SEL_GROUP) // SEL_GROUP

    def sel_scores(c):
        kpos = c * T + rows
        chosen = sel_ref[pl.ds(c * blocks_per_tile, 1), :]
        for i in range(1, blocks_per_tile):
            chosen = jnp.where(rows < i * SEL_BLOCK, chosen, sel_ref[pl.ds(c * blocks_per_tile + i, 1), :])
        return scores(0, c, (chosen > 0.5) & (kpos <= t_q))

    def sel_pass1(i, m_run):
        ids = [SEL_GROUP * i + u for u in range(SEL_GROUP)]
        tiles = [sel_scores(c) for c in ids]
        for c, s in zip(ids, tiles):
            ss_ref[c] = s
        return jnp.maximum(m_run, col_max(tiles))

    m_s = lax.fori_loop(0, trips, sel_pass1, jnp.full((1, R * T), NEG_INF, jnp.float32))
    acc_ref[...] = jnp.zeros(acc_ref.shape, jnp.float32)

    def sel_pass2(i, l_run):
        ids = [SEL_GROUP * i + u for u in range(SEL_GROUP)]
        pv, total = weighted_values(0, ids, [jnp.exp(ss_ref[c] - m_s) for c in ids])
        acc_ref[...] += pv
        return l_run + total

    l_s = lax.fori_loop(0, trips, sel_pass2, jnp.zeros((1, R * T), jnp.float32))
    o_s = acc_ref[...] * (1.0 / l_s)

    gate_ref[...] = _sigmoid(sm_ref[...]).T
    outs = []
    for r in range(R):
        def gate(branch):
            return gate_ref[pl.ds(SM_GATE + branch * NSA_HEADS + g * R + r, 1), :]
        sl = slice(r * T, (r + 1) * T)
        outs.append((gate(0) * o_c[:, sl] + gate(1) * o_s[:, sl] + gate(2) * o_w[:, sl]).T)
    o_ref[...] = jnp.concatenate(outs, axis=1).astype(o_ref.dtype)


def _nsa(z, zs, k_c, v_c, bias_c, bias_t, batch, seq):
    hb = HEAD_DIM
    R = NSA_GROUP_SIZE
    nblk = seq // CMP_STRIDE
    nq = seq // TILE
    gw = R * hb
    assert OFF_NQ % gw == 0

    def seg(off):
        return pl.BlockSpec((seq, hb), lambda b, g, off=off: (b, off // hb + g))

    return pl.pallas_call(
        functools.partial(_nsa_kernel, seq=seq),
        grid=(batch, NSA_KV_GROUPS),
        in_specs=[pl.BlockSpec((seq, gw), lambda b, g: (b, OFF_NQ // gw + g)),
                  pl.BlockSpec((1, 1, nblk, hb), lambda b, g: (b, g, 0, 0)),
                  pl.BlockSpec((1, 1, hb, nblk), lambda b, g: (b, g, 0, 0)),
                  seg(OFF_NKS), seg(OFF_NVS), seg(OFF_NKW), seg(OFF_NVW),
                  pl.BlockSpec((seq, SMALL_COLS), lambda b, g: (b, 0)),
                  pl.BlockSpec((R, nq, nblk, TILE), lambda b, g: (g, 0, 0, 0)),
                  pl.BlockSpec((3, R, TILE, TILE), lambda b, g: (0, g, 0, 0))],
        out_specs=pl.BlockSpec((seq, gw), lambda b, g: (b, g)),
        out_shape=jax.ShapeDtypeStruct((batch * seq, NSA_WIDTH), jnp.bfloat16),
        scratch_shapes=[pltpu.VMEM((2, nq, TILE, hb), jnp.bfloat16),
                        pltpu.VMEM((2, nq, hb, TILE), jnp.bfloat16),
                        pltpu.VMEM((hb, R * TILE), jnp.float32),
                        pltpu.VMEM((WINDOW // TILE + 1, TILE, R * TILE), jnp.float32),
                        pltpu.VMEM((nq, TILE, R * TILE), jnp.float32),
                        pltpu.VMEM((seq // SEL_BLOCK, TILE), jnp.float32),
                        pltpu.VMEM((SMALL_COLS, TILE), jnp.float32)],
        compiler_params=_cparams(("parallel", "parallel")),
        name="nsa_attention",
    )(z, k_c, v_c, z, z, z, z, zs, bias_c, bias_t)


def _rope_tables(seq):
    half = HEAD_DIM // 2
    inv_freq = 1.0 / (10000.0 ** jnp.linspace(0.0, 1.0, half))
    ang = jnp.arange(seq).astype(jnp.float32)[:, None] * inv_freq[None, :]
    cos, sin = jnp.cos(ang), jnp.sin(ang)
    return jnp.concatenate([cos, cos], axis=-1), jnp.concatenate([-sin, sin], axis=-1)


def _retention_log_decay():
    lg = np.log1p(-np.exp2(-5.0 - np.arange(RET_HEADS, dtype=np.float64))).astype(np.float32)
    return jnp.asarray(np.broadcast_to(lg[:, None, None], (RET_HEADS, 1, HEAD_DIM)).copy())


def _split_in_proj(w_in, b_in, w_gate_down):
    bf = jnp.bfloat16
    a0 = 4 * MLSTM_WIDTH
    a1 = a0 + 2 * MLSTM_HEADS
    a2 = a1 + NSA_WIDTH + 6 * NSA_KV_WIDTH
    a3 = a2 + 3 * NSA_HEADS
    depth, d, _ = w_in.shape
    w_main = jnp.concatenate([w_gate_down.astype(bf), w_in[:, :, :a0].astype(bf), w_in[:, :, a1:a2].astype(bf),
                              w_in[:, :, a3:].astype(bf)], axis=2)
    b_main = jnp.concatenate([jnp.zeros((depth, GATE_RANK), jnp.float32), b_in[:, :a0], b_in[:, a1:a2],
                              b_in[:, a3:]], axis=1)
    n_small = (a1 - a0) + (a3 - a2)
    w_small = jnp.concatenate([w_in[:, :, a0:a1].astype(bf), w_in[:, :, a2:a3].astype(bf),
                               jnp.zeros((depth, d, SMALL_COLS - n_small), bf)], axis=2)
    b_small = jnp.concatenate([b_in[:, a0:a1], b_in[:, a2:a3],
                               jnp.zeros((depth, SMALL_COLS - n_small), jnp.float32)], axis=1)
    return w_main, b_main, w_small, b_small


def kernel(x, rel_bias, norm_mix_g, w_in, b_in, conv_qk, mlstm_norm_g, cmp_pos_k, cmp_w1_k, cmp_w2_k, cmp_pos_v, cmp_w1_v, cmp_w2_v, ret_norm_g, w_br_mlstm, w_br_nsa, w_br_ret, w_gate_down, w_gate_up, b_gate, w_out, norm_mlp_g, w_up, w_down, final_norm_g):
    batch, seq, d = x.shape
    depth = w_in.shape[0]
    bf = jnp.bfloat16
    f32 = jnp.float32
    bias_c, bias_t = _bias_tables(rel_bias, seq)
    cosf, sinf = _rope_tables(seq)
    lg = _retention_log_decay()
    w_main, b_main, w_small, b_small = _split_in_proj(w_in, b_in, w_gate_down)
    w_out_bf = w_out.astype(bf)
    w1_k, w2_k, w1_v, w2_vt = cmp_w1_k.astype(bf), cmp_w2_k.astype(bf), cmp_w1_v.astype(bf), jnp.swapaxes(cmp_w2_v, 1, 2).astype(bf)
    xr = x.reshape(batch * seq, d)
    for l in range(depth):
        h = _rmsnorm(xr, norm_mix_g[l], bf)
        z = _matmul(h, w_main, l, bias=b_main, out_dtype=f32, tm=1024, tn=1280, tk=D_MODEL, name="in_proj")
        zs = _matmul(h, w_small, l, bias=b_small, out_dtype=f32, tm=1024, tn=SMALL_COLS, tk=D_MODEL,
                     name="in_proj_small")
        y_a = _mlstm(z, zs, conv_qk[l], mlstm_norm_g[l], batch, seq)
        k_c, v_c = _compress(z, cmp_pos_k[l], w1_k[l], w2_k[l], cmp_pos_v[l], w1_v[l], w2_vt[l], batch, seq)
        y_b = _nsa(z, zs, k_c, v_c, bias_c, bias_t, batch, seq)
        y_c = _retention(z, cosf, sinf, lg, ret_norm_g[l], batch, seq)
        merged = _merge(z, y_a, y_b, y_c, l, w_gate_up, b_gate, w_br_mlstm, w_br_nsa, w_br_ret)
        xr, w_up_l = _matmul(merged, w_out_bf, l, res=xr, out_dtype=f32, tm=1024, tn=512, tk=D_MODEL, name="out_proj",
                             side=w_up)
        h = _rmsnorm(xr, norm_mlp_g[l], bf)
        up, w_down_l = _matmul(h, w_up_l, l, act="relu2", out_dtype=bf, tm=512, tn=2048, tk=D_MODEL, name="mlp_up",
                               side=w_down)
        xr = _matmul(up, w_down_l, None, res=xr, out_dtype=f32, tm=1024, tn=1024, tk=4096, name="mlp_down")
    return _rmsnorm(xr, final_norm_g, f32).reshape(batch, seq, d)
```
